```python
import math
import jax, jax.numpy as jnp
from jax import lax
import numpy as np

D_MODEL = 1024
BATCH = 16
SEQ = 2048
DEPTH = 1

HEAD_DIM = 64
ATTN_SCALE = HEAD_DIM ** -0.5
NSA_HEADS = 8
NSA_KV = 2
NSA_REP = NSA_HEADS // NSA_KV
NSA_WIDTH = NSA_HEADS * HEAD_DIM
CMP_LEN = 32
CMP_STRIDE = 16
CMP_HIDDEN = 256
SEL_LEN = 64
SEL_TOPN = 8
NSA_WINDOW = 512
SWA_HEADS = 8
SWA_KV = 2
SWA_REP = SWA_HEADS // SWA_KV
SWA_WIDTH = SWA_HEADS * HEAD_DIM
SWA_WINDOW = 128
Q_BLOCK = 128
D_FF = 2816
FFN_RES = 0.5
RMS_EPS = 1e-6
NEG_INF = -1e30
SEL_BONUS = 1e4
IN_SIZES = (NSA_WIDTH, 2 * NSA_KV * HEAD_DIM, 2 * NSA_KV * HEAD_DIM, 2 * NSA_KV * HEAD_DIM,
            3 * NSA_HEADS, SWA_WIDTH, 2 * SWA_KV * HEAD_DIM, 2 * D_MODEL)
IN_WIDTH = NSA_WIDTH + 6 * NSA_KV * HEAD_DIM + 3 * NSA_HEADS + SWA_WIDTH + 2 * SWA_KV * HEAD_DIM + 2 * D_MODEL

kernel_name = "hybrid_nsa_swa_sink_macaron_adaln"


def alibi_slopes(n):
    return np.array([2.0 ** (-8.0 * (h + 1) / n) for h in range(n)], dtype=np.float32)


def rms_norm(x, g):
    xf = x.astype(jnp.float32)
    y = xf * lax.rsqrt(jnp.mean(xf * xf, axis=-1, keepdims=True) + RMS_EPS)
    return y.astype(x.dtype) * g


def modulate(h, g, shift, scale):
    return rms_norm(h, g) * (1.0 + scale[:, None, :]) + shift[:, None, :]


def swiglu(u, wg, wu, wd):
    return (jax.nn.silu(u @ wg) * (u @ wu)) @ wd


def split_cols(a, sizes):
    outs, off = [], 0
    for n in sizes:
        outs.append(a[..., off:off + n])
        off += n
    return outs


def split_kv(a, groups):
    B, S, _ = a.shape
    a = a.reshape(B, S, 2, groups, HEAD_DIM)
    return a[:, :, 0], a[:, :, 1]


def banded_attention(q, k, v, slopes, window, sinks=None):
    B, S, KH, R, dh = q.shape
    nb = S // Q_BLOCK
    pad = -(-window // Q_BLOCK) * Q_BLOCK
    span = pad + Q_BLOCK
    kp = jnp.pad(k, ((0, 0), (pad, 0), (0, 0), (0, 0)))
    vp = jnp.pad(v, ((0, 0), (pad, 0), (0, 0), (0, 0)))
    qb = jnp.moveaxis(q.reshape(B, nb, Q_BLOCK, KH, R, dh), 1, 0)
    sl = jnp.asarray(slopes)[:, :, None, None]

    def one(args):
        b, qblk = args
        start = b * Q_BLOCK
        kblk = lax.dynamic_slice_in_dim(kp, start, span, axis=1)
        vblk = lax.dynamic_slice_in_dim(vp, start, span, axis=1)
        q_pos = start + jnp.arange(Q_BLOCK)
        k_pos = start - pad + jnp.arange(span)
        dist = q_pos[:, None] - k_pos[None, :]
        valid = (dist >= 0) & (dist < window) & (k_pos[None, :] >= 0)
        s = jnp.einsum('bqkrd,bskd->bkrqs', qblk, kblk).astype(jnp.float32) * ATTN_SCALE
        s = s - sl * dist.astype(jnp.float32)
        s = jnp.where(valid, s, NEG_INF)
        if sinks is not None:
            sink = jnp.broadcast_to(sinks.astype(jnp.float32)[None, :, :, None, None], (B, KH, R, Q_BLOCK, 1))
            p = jax.nn.softmax(jnp.concatenate([s, sink], axis=-1), axis=-1)[..., :-1]
        else:
            p = jax.nn.softmax(s, axis=-1)
        return jnp.einsum('bkrqs,bskd->bqkrd', p.astype(v.dtype), vblk)

    o = lax.map(one, (jnp.arange(nb), qb))
    return jnp.moveaxis(o, 0, 1).reshape(B, S, KH, R, dh)


def nsa_compress(kraw, pos, w1, w2):
    S = kraw.shape[1]
    nc = (S - CMP_LEN) // CMP_STRIDE + 1
    idx = np.arange(nc)[:, None] * CMP_STRIDE + np.arange(CMP_LEN)[None, :]
    blocks = kraw[:, idx] + pos[None, None, :, None, :]
    h = jax.nn.gelu(jnp.einsum('bnlgd,ldh->bngh', blocks, w1))
    return jnp.einsum('bngh,hd->bngd', h, w2)


def selected_attention(q, k, v, idx, slopes):
    B, S, G, R, dh = q.shape
    n = idx.shape[-1]
    nqb = S // SEL_LEN
    qb = jnp.moveaxis(q.reshape(B, nqb, SEL_LEN, G, R, dh), 1, 0)
    ib = jnp.moveaxis(idx.reshape(B, nqb, SEL_LEN, G, n), 1, 0)
    bi = jnp.arange(B)[:, None, None, None]
    gi = jnp.arange(G)[None, None, :, None]
    offs = jnp.arange(SEL_LEN)
    sl = jnp.asarray(slopes)[None, None, :, :, None]

    def one(args):
        b, qblk, iblk = args
        t = b * SEL_LEN + jnp.arange(SEL_LEN)
        tok = (iblk[..., None] * SEL_LEN + offs).reshape(B, SEL_LEN, G, n * SEL_LEN)
        ks = k[bi, tok, gi]
        vs = v[bi, tok, gi]
        dist = (t[None, :, None, None] - tok)[:, :, :, None, :]
        s = jnp.einsum('bqgrd,bqgsd->bqgrs', qblk, ks).astype(jnp.float32) * ATTN_SCALE
        s = s - sl * dist.astype(jnp.float32)
        s = jnp.where(dist >= 0, s, NEG_INF)
        p = jax.nn.softmax(s, axis=-1)
        return jnp.einsum('bqgrs,bqgsd->bqgrd', p.astype(v.dtype), vs)

    o = lax.map(one, (jnp.arange(nqb), qb, ib))
    return jnp.moveaxis(o, 0, 1).reshape(B, S, G, R, dh)


def nsa_attention(q, kv_c, kv_s, kv_w, gates, pos_k, w1_k, w2_k, pos_v, w1_v, w2_v):
    B, S = q.shape[0], q.shape[1]
    q = q.reshape(B, S, NSA_KV, NSA_REP, HEAD_DIM)
    slopes = alibi_slopes(NSA_HEADS).reshape(NSA_KV, NSA_REP)
    kc_raw, vc_raw = split_kv(kv_c, NSA_KV)
    k_s, v_s = split_kv(kv_s, NSA_KV)
    k_w, v_w = split_kv(kv_w, NSA_KV)
    kc = nsa_compress(kc_raw, pos_k, w1_k, w2_k)
    vc = nsa_compress(vc_raw, pos_v, w1_v, w2_v)
    nc = kc.shape[1]
    t = np.arange(S)
    cmp_start = np.arange(nc) * CMP_STRIDE
    dist_c = t[:, None] - (cmp_start + CMP_LEN - 1)[None, :]
    pen_c = (slopes[:, :, None, None] * dist_c[None, None]).astype(np.float32)
    s = jnp.einsum('bsgrd,bngd->bgrsn', q, kc).astype(jnp.float32) * ATTN_SCALE - pen_c
    p_cmp = jax.nn.softmax(jnp.where(dist_c >= 0, s, NEG_INF), axis=-1)
    p_cmp = jnp.where((t >= CMP_LEN - 1)[:, None], p_cmp, 0.0)
    o_cmp = jnp.einsum('bgrsn,bngd->bsgrd', p_cmp.astype(vc.dtype), vc)
    nsel = S // SEL_LEN
    n_top = min(SEL_TOPN, nsel)
    sel_start = np.arange(nsel) * SEL_LEN
    overlap = ((cmp_start[:, None] < sel_start[None, :] + SEL_LEN)
               & (cmp_start[:, None] + CMP_LEN > sel_start[None, :])).astype(np.float32)
    imp = jnp.einsum('bgrsn,nj->bsgj', p_cmp, overlap)
    cur = t // SEL_LEN
    jj = np.arange(nsel)
    valid_sel = (sel_start[None, :] <= t[:, None])[:, None, :]
    forced = ((jj[None, :] == 0) | (jj[None, :] == cur[:, None]) | (jj[None, :] == cur[:, None] - 1))[:, None, :]
    score = jnp.where(forced, SEL_BONUS, jnp.where(valid_sel, imp, -1.0))
    _, idx = lax.top_k(score, n_top)
    o_slc = selected_attention(q, k_s, v_s, idx, slopes)
    o_win = banded_attention(q, k_w, v_w, slopes, NSA_WINDOW)
    g = jax.nn.sigmoid(gates.reshape(B, S, NSA_KV, NSA_REP, 3))
    o = g[..., 0:1] * o_cmp + g[..., 1:2] * o_slc + g[..., 2:3] * o_win
    return o.reshape(B, S, NSA_WIDTH)


def swa_sink_attention(q, kv, sinks):
    B, S = q.shape[0], q.shape[1]
    q = q.reshape(B, S, SWA_KV, SWA_REP, HEAD_DIM)
    k, v = split_kv(kv, SWA_KV)
    slopes = alibi_slopes(SWA_HEADS).reshape(SWA_KV, SWA_REP)
    o = banded_attention(q, k, v, slopes, SWA_WINDOW, sinks=sinks)
    return o.reshape(B, S, SWA_WIDTH)


def hybrid_mixer(u, w_in, cmp_pos_k, cmp_w1_k, cmp_w2_k, cmp_pos_v, cmp_w1_v, cmp_w2_v,
                 sinks, w_up_a, w_up_b, w_out):
    proj = u @ w_in
    q_n, kv_c, kv_s, kv_w, g_n, q_s, kv_b, g_m = split_cols(proj, IN_SIZES)
    y_a = nsa_attention(q_n, kv_c, kv_s, kv_w, g_n, cmp_pos_k, cmp_w1_k, cmp_w2_k,
                        cmp_pos_v, cmp_w1_v, cmp_w2_v)
    y_b = swa_sink_attention(q_s, kv_b, sinks)
    gate_a, gate_b = g_m[..., :D_MODEL], g_m[..., D_MODEL:]
    merged = jax.nn.sigmoid(gate_a) * (y_a @ w_up_a) + jax.nn.sigmoid(gate_b) * (y_b @ w_up_b)
    return merged @ w_out


def setup_inputs(seed: int = 0) -> dict:
    key = jax.random.key(seed)
    ks = jax.random.split(key, 25)
    L, D = DEPTH, D_MODEL

    def nrm(k, shape, scale):
        return jax.random.normal(k, shape, jnp.float32) * scale

    return {
        "x": nrm(ks[0], (BATCH, SEQ, D), 1.0),
        "c": nrm(ks[1], (BATCH, D), 1.0),
        "w_ada": nrm(ks[2], (L, D, 9 * D), D ** -0.5),
        "b_ada": nrm(ks[3], (L, 9 * D), 0.01),
        "g_ffn1": 1.0 + nrm(ks[4], (L, D), 0.02),
        "w1_gate": nrm(ks[5], (L, D, D_FF), D ** -0.5),
        "w1_up": nrm(ks[6], (L, D, D_FF), D ** -0.5),
        "w1_down": nrm(ks[7], (L, D_FF, D), D_FF ** -0.5),
        "g_mix": 1.0 + nrm(ks[8], (L, D), 0.02),
        "w_in": nrm(ks[9], (L, D, IN_WIDTH), D ** -0.5),
        "cmp_pos_k": nrm(ks[10], (L, CMP_LEN, HEAD_DIM), 0.1),
        "cmp_w1_k": nrm(ks[11], (L, CMP_LEN, HEAD_DIM, CMP_HIDDEN), (CMP_LEN * HEAD_DIM) ** -0.5),
        "cmp_w2_k": nrm(ks[12], (L, CMP_HIDDEN, HEAD_DIM), CMP_HIDDEN ** -0.5),
        "cmp_pos_v": nrm(ks[13], (L, CMP_LEN, HEAD_DIM), 0.1),
        "cmp_w1_v": nrm(ks[14], (L, CMP_LEN, HEAD_DIM, CMP_HIDDEN), (CMP_LEN * HEAD_DIM) ** -0.5),
        "cmp_w2_v": nrm(ks[15], (L, CMP_HIDDEN, HEAD_DIM), CMP_HIDDEN ** -0.5),
        "sinks": nrm(ks[16], (L, SWA_KV, SWA_REP), 1.0),
        "w_up_a": nrm(ks[17], (L, NSA_WIDTH, D), NSA_WIDTH ** -0.5),
        "w_up_b": nrm(ks[18], (L, SWA_WIDTH, D), SWA_WIDTH ** -0.5),
        "w_out": nrm(ks[19], (L, D, D), D ** -0.5),
        "g_ffn2": 1.0 + nrm(ks[20], (L, D), 0.02),
        "w2_gate": nrm(ks[21], (L, D, D_FF), D ** -0.5),
        "w2_up": nrm(ks[22], (L, D, D_FF), D ** -0.5),
        "w2_down": nrm(ks[23], (L, D_FF, D), D_FF ** -0.5),
        "g_final": 1.0 + nrm(ks[24], (D,), 0.02),
    }


def reference(x, c, w_ada, b_ada, g_ffn1, w1_gate, w1_up, w1_down, g_mix, w_in,
              cmp_pos_k, cmp_w1_k, cmp_w2_k, cmp_pos_v, cmp_w1_v, cmp_w2_v, sinks,
              w_up_a, w_up_b, w_out, g_ffn2, w2_gate, w2_up, w2_down, g_final):
    h = x
    for l in range(DEPTH):
        mod = jax.nn.silu(c) @ w_ada[l] + b_ada[l]
        sh1, sc1, gt1, sh2, sc2, gt2, sh3, sc3, gt3 = jnp.split(mod, 9, axis=-1)
        u = modulate(h, g_ffn1[l], sh1, sc1)
        h = h + FFN_RES * gt1[:, None, :] * swiglu(u, w1_gate[l], w1_up[l], w1_down[l])
        u = modulate(h, g_mix[l], sh2, sc2)
        y = hybrid_mixer(u, w_in[l], cmp_pos_k[l], cmp_w1_k[l], cmp_w2_k[l], cmp_pos_v[l],
                         cmp_w1_v[l], cmp_w2_v[l], sinks[l], w_up_a[l], w_up_b[l], w_out[l])
        h = h + gt2[:, None, :] * y
        u = modulate(h, g_ffn2[l], sh3, sc3)
        h = h + FFN_RES * gt3[:, None, :] * swiglu(u, w2_gate[l], w2_up[l], w2_down[l])
    return rms_norm(h, g_final)
```

```python
import functools

import numpy as np
import jax
import jax.numpy as jnp
from jax import lax
from jax.experimental import pallas as pl
from jax.experimental.pallas import tpu as pltpu

F32 = jnp.float32
BF16 = jnp.bfloat16

HEAD_DIM = 64
N_HEADS = 8
N_KV = 2
N_REP = N_HEADS // N_KV
CMP_LEN = 32
CMP_STRIDE = 16
CMP_HIDDEN = 256
SEL_LEN = 64
SEL_TOPN = 8
NSA_WINDOW = 512
SWA_WINDOW = 128
FFN_RES = 0.5
RMS_EPS = 1e-6
NEG_INF = -1e30
SEL_BONUS = 1e4
ATTN_SCALE = HEAD_DIM ** -0.5

LANES = 128
VMEM_LIMIT = 56 * 1024 * 1024

SLOPES = [2.0 ** (-8.0 * (h + 1) / N_HEADS) for h in range(N_HEADS)]


def _const_spec(shape):
    n = len(shape)
    return pl.BlockSpec(shape, lambda *_: (0,) * n, pipeline_mode=pl.Buffered(1))


def _params(sem):
    return pltpu.CompilerParams(dimension_semantics=sem, vmem_limit_bytes=VMEM_LIMIT)


def _modulated_norm(x, g, shift, scale):
    ms = jnp.mean(x * x, axis=-1, keepdims=True)
    y = x * lax.rsqrt(ms + RMS_EPS)
    return (y * g) * (1.0 + scale) + shift


def _split3(a):
    hi = a.astype(BF16)
    r1 = a - hi.astype(F32)
    mid = r1.astype(BF16)
    lo = (r1 - mid.astype(F32)).astype(BF16)
    return hi, mid, lo


def _ada_kernel(c_ref, w_ref, b_ref, o_ref):
    c = c_ref[...]
    a = c * jax.nn.sigmoid(c)
    a_hi = a.astype(BF16)
    a_lo = (a - a_hi.astype(F32)).astype(BF16)
    w = w_ref[...]
    w_hi = w.astype(BF16)
    w_lo = (w - w_hi.astype(F32)).astype(BF16)
    acc = jnp.dot(a_hi, w_hi, preferred_element_type=F32)
    acc += jnp.dot(a_hi, w_lo, preferred_element_type=F32)
    acc += jnp.dot(a_lo, w_hi, preferred_element_type=F32)
    o_ref[...] = acc + b_ref[...]


def _ada_call(c, w, b):
    bsz, d = c.shape
    n = w.shape[1]
    tn = 1024
    return pl.pallas_call(
        _ada_kernel,
        grid=(n // tn,),
        in_specs=[pl.BlockSpec((bsz, d), lambda j: (0, 0)),
                  pl.BlockSpec((d, tn), lambda j: (0, j)),
                  pl.BlockSpec((1, tn), lambda j: (0, j))],
        out_specs=pl.BlockSpec((bsz, tn), lambda j: (0, j)),
        out_shape=jax.ShapeDtypeStruct((bsz, n), F32),
        compiler_params=_params(("parallel",)),
        name="adaln",
    )(c, w, b.reshape(1, n))


def _ffn_kernel(x_ref, sh_ref, sc_ref, gt_ref, g_ref, wg_ref, wu_ref, wd_ref, *rest, tf, final):
    o_ref = rest[-1]
    x = x_ref[0]
    u = _modulated_norm(x, g_ref[...], sh_ref[0], sc_ref[0]).astype(BF16)
    dff = wg_ref.shape[1]
    acc = None
    for c in range(dff // tf):
        cols = slice(c * tf, (c + 1) * tf)
        gate = jnp.dot(u, wg_ref[:, cols], preferred_element_type=F32)
        up = jnp.dot(u, wu_ref[:, cols], preferred_element_type=F32)
        act = (gate * jax.nn.sigmoid(gate) * up).astype(BF16)
        part = jnp.dot(act, wd_ref[cols, :], preferred_element_type=F32)
        acc = part if acc is None else acc + part
    h = x + (FFN_RES * gt_ref[0]) * acc
    if final:
        gfin_ref = rest[0]
        ms = jnp.mean(h * h, axis=-1, keepdims=True)
        h = (h * lax.rsqrt(ms + RMS_EPS)) * gfin_ref[...]
    o_ref[0] = h


def _ffn_call(h, shift, scale, gate, g, wg, wu, wd, g_final=None, *, tm=512, tf=256):
    bsz, s, d = h.shape
    dff = wg.shape[1]
    final = g_final is not None
    vec = pl.BlockSpec((1, 1, d), lambda b, i: (b, 0, 0))
    in_specs = [pl.BlockSpec((1, tm, d), lambda b, i: (b, i, 0)), vec, vec, vec,
                _const_spec((1, d)), _const_spec((d, dff)), _const_spec((d, dff)), _const_spec((dff, d))]
    args = [h, shift, scale, gate, g.reshape(1, d), wg, wu, wd]
    if final:
        in_specs.append(_const_spec((1, d)))
        args.append(g_final.reshape(1, d))
    return pl.pallas_call(
        functools.partial(_ffn_kernel, tf=tf, final=final),
        grid=(bsz, s // tm),
        in_specs=in_specs,
        out_specs=pl.BlockSpec((1, tm, d), lambda b, i: (b, i, 0)),
        out_shape=jax.ShapeDtypeStruct((bsz, s, d), F32),
        compiler_params=_params(("parallel", "parallel")),
        name="ffn_final" if final else "ffn",
    )(*args)


N_KVSLOT = 4 * N_KV
Q_COLS = N_HEADS * HEAD_DIM
PROJ_COLS = 2 * Q_COLS + N_KVSLOT * LANES + LANES


def _proj_kernel(x_ref, sh_ref, sc_ref, g_ref, w_ref, qn_ref, qs_ref, kv_ref, gn_ref):
    u = _modulated_norm(x_ref[0], g_ref[...], sh_ref[0], sc_ref[0]).astype(BF16)
    proj = jnp.dot(u, w_ref[...], preferred_element_type=F32)
    qn_ref[0] = (proj[:, 0:Q_COLS] * ATTN_SCALE).astype(BF16)
    qs_ref[0] = (proj[:, Q_COLS:2 * Q_COLS] * ATTN_SCALE).astype(BF16)
    base = 2 * Q_COLS
    for i in range(N_KVSLOT):
        kv_ref[0, i] = proj[:, base + i * LANES: base + (i + 1) * LANES].astype(BF16)
    gn_ref[0] = proj[:, base + N_KVSLOT * LANES:]


def _proj_call(h, shift, scale, g, w, *, tm=512):
    bsz, s, d = h.shape
    vec = pl.BlockSpec((1, 1, d), lambda b, i: (b, 0, 0))
    return pl.pallas_call(
        _proj_kernel,
        grid=(bsz, s // tm),
        in_specs=[pl.BlockSpec((1, tm, d), lambda b, i: (b, i, 0)), vec, vec,
                  _const_spec((1, d)), _const_spec((d, PROJ_COLS))],
        out_specs=[pl.BlockSpec((1, tm, Q_COLS), lambda b, i: (b, i, 0)),
                   pl.BlockSpec((1, tm, Q_COLS), lambda b, i: (b, i, 0)),
                   pl.BlockSpec((1, N_KVSLOT, tm, LANES), lambda b, i: (b, 0, i, 0)),
                   pl.BlockSpec((1, tm, LANES), lambda b, i: (b, i, 0))],
        out_shape=[jax.ShapeDtypeStruct((bsz, s, Q_COLS), BF16),
                   jax.ShapeDtypeStruct((bsz, s, Q_COLS), BF16),
                   jax.ShapeDtypeStruct((bsz, N_KVSLOT, s, LANES), BF16),
                   jax.ShapeDtypeStruct((bsz, s, LANES), F32)],
        compiler_params=_params(("parallel", "parallel")),
        name="mixer_proj",
    )(h, shift, scale, g.reshape(1, d), w)


def _cmp_kernel(a_ref, pa_ref, pb_ref, w1a_ref, w1b_ref, w2_ref, o_ref):
    a = a_ref[0, 0]
    first = jnp.dot(a, w1a_ref[...], preferred_element_type=F32)
    second = jnp.dot(a, w1b_ref[...], preferred_element_type=F32)
    bias = (jnp.dot(pa_ref[...], w1a_ref[...], preferred_element_type=F32)
            + jnp.dot(pb_ref[...], w1b_ref[...], preferred_element_type=F32))[0:1]
    n = a.shape[0]
    hid = first + pltpu.roll(second, n - 1, axis=0) + bias
    hid = jax.nn.gelu(hid)
    o_ref[0, 0] = jnp.dot(hid.astype(BF16), w2_ref[...], preferred_element_type=F32).astype(BF16)


def _cmp_call(kv_chunks, pa, pb, w1a, w1b, w2):
    bsz, _, nchunk, width = kv_chunks.shape
    return pl.pallas_call(
        _cmp_kernel,
        grid=(bsz, N_KV),
        in_specs=[pl.BlockSpec((1, 1, nchunk, width), lambda b, g: (b, g, 0, 0)),
                  _const_spec(pa.shape), _const_spec(pb.shape),
                  _const_spec(w1a.shape), _const_spec(w1b.shape), _const_spec(w2.shape)],
        out_specs=pl.BlockSpec((1, 1, nchunk, LANES), lambda b, g: (b, g, 0, 0)),
        out_shape=jax.ShapeDtypeStruct((bsz, N_KV, nchunk, LANES), BF16),
        compiler_params=_params(("parallel", "parallel")),
        name="nsa_compress",
    )(kv_chunks, pa, pb, w1a, w1b, w2)


def _attn_kernel(sinks_ref, qn_ref, qs_ref, gn_ref, kv_ref, kvc_ref, ya_ref, yb_ref,
                 m_scr, l_scr, acc_scr, *, seq, tq, tk, tk_swa):
    qi = pl.program_id(1)
    t0 = qi * tq
    nsel = seq // SEL_LEN
    ncmp = kvc_ref.shape[2]
    n_top = min(SEL_TOPN, nsel)
    lane = lax.broadcasted_iota(jnp.int32, (tq, LANES), 1)

    def head_q(q_ref, h):
        pair = q_ref[0, :, (h // 2) * LANES:(h // 2 + 1) * LANES].astype(F32)
        if h % 2:
            pair = pltpu.roll(pair, HEAD_DIM, axis=1)
        return jnp.where(lane < HEAD_DIM, pair, 0.0).astype(BF16)

    def nt_dot(a, b):
        return lax.dot_general(a, b, (((1,), (1,)), ((), ())), preferred_element_type=F32)

    def flash(qs4, slot, lo, hi, blk, mask_fn, slopes4, sink4=None):
        for r in range(N_REP):
            if sink4 is None:
                m_scr[r] = jnp.full((tq, LANES), NEG_INF, F32)
                l_scr[r] = jnp.zeros((tq, LANES), F32)
            else:
                m_scr[r] = jnp.full((tq, LANES), sink4[r], F32)
                l_scr[r] = jnp.ones((tq, LANES), F32)
            acc_scr[r] = jnp.zeros((tq, LANES), F32)

        def body(kb, carry):
            k0 = pl.multiple_of(kb * blk, blk)
            kv = kv_ref[0, slot, pl.ds(k0, blk), :]
            mask, relpos = mask_fn(k0, blk)
            for r in range(N_REP):
                s = nt_dot(qs4[r], kv)
                s = jnp.where(mask, s + slopes4[r] * relpos, NEG_INF)
                m_prev = m_scr[r]
                m_new = jnp.maximum(m_prev, jnp.max(s, axis=1, keepdims=True))
                alpha = jnp.exp(m_prev - m_new)
                p = jnp.exp(s - pltpu.repeat(m_new, blk // LANES, 1))
                l_scr[r] = alpha * l_scr[r] + jnp.sum(p, axis=1, keepdims=True)
                acc_scr[r] = alpha * acc_scr[r] + jnp.dot(p.astype(BF16), kv, preferred_element_type=F32)
                m_scr[r] = m_new
            return carry

        lax.fori_loop(lo, hi, body, 0)
        return [acc_scr[r] / l_scr[r] for r in range(N_REP)]

    def pos_grids(k0, blk):
        key = k0 + lax.broadcasted_iota(jnp.int32, (tq, blk), 1)
        t = t0 + lax.broadcasted_iota(jnp.int32, (tq, blk), 0)
        return key, t

    def band_mask(window):
        def fn(k0, blk):
            key, t = pos_grids(k0, blk)
            d = t - key
            return (d >= 0) & (d < window), (-d).astype(F32)
        return fn

    def pack_heads(o4):
        pairs = []
        for p in range(N_REP // 2):
            left = pltpu.roll(o4[2 * p], HEAD_DIM, axis=1)
            pairs.append(jnp.where(lane < HEAD_DIM, left, o4[2 * p + 1]))
        return jnp.concatenate(pairs, axis=1)

    gsig = jax.nn.sigmoid(gn_ref[0])

    def gate_col(c):
        return jnp.sum(jnp.where(lane == c, gsig, 0.0), axis=1, keepdims=True)

    last_blk = (t0 + tq - 1) // tk + 1

    for g in range(N_KV):
        slopes4 = SLOPES[g * N_REP:(g + 1) * N_REP]
        qs4 = [head_q(qn_ref, g * N_REP + r) for r in range(N_REP)]

        kvc = kvc_ref[0, g]
        cidx = lax.broadcasted_iota(jnp.int32, (tq, ncmp), 1)
        cend = cidx * CMP_STRIDE + (CMP_LEN - 1)
        tc = t0 + lax.broadcasted_iota(jnp.int32, (tq, ncmp), 0)
        cdist = tc - cend
        cmask = cdist >= 0
        crel = (-cdist).astype(F32)
        row_ok = tc >= (CMP_LEN - 1)
        psum = jnp.zeros((tq, ncmp), F32)
        o_cmp = []
        for r in range(N_REP):
            s = nt_dot(qs4[r], kvc)
            s = jnp.where(cmask, s + slopes4[r] * crel, NEG_INF)
            e = jnp.exp(s - jnp.max(s, axis=1, keepdims=True))
            p = e / jnp.sum(e, axis=1, keepdims=True)
            p = jnp.where(row_ok, p, 0.0)
            psum = psum + p
            o_cmp.append(jnp.dot(p.astype(BF16), kvc, preferred_element_type=F32))

        jn = lax.broadcasted_iota(jnp.int32, (nsel, ncmp), 0) * SEL_LEN
        cn = lax.broadcasted_iota(jnp.int32, (nsel, ncmp), 1) * CMP_STRIDE
        ov_t = jnp.where((cn < jn + SEL_LEN) & (cn + CMP_LEN > jn), 1.0, 0.0).astype(BF16)
        imp_t = sum(nt_dot(ov_t, part) for part in _split3(psum))
        jb = lax.broadcasted_iota(jnp.int32, (nsel, tq), 0)
        cur = (t0 + lax.broadcasted_iota(jnp.int32, (nsel, tq), 1)) // SEL_LEN
        forced = (jb == 0) | (jb == cur) | (jb == cur - 1)
        score = jnp.where(forced, SEL_BONUS, jnp.where(jb <= cur, imp_t, -1.0))
        rank = jnp.zeros((nsel, tq), F32)
        for i in range(nsel):
            row = score[i:i + 1, :]
            gt = jnp.where(row > score, 1.0, 0.0)
            ge = jnp.where(row >= score, 1.0, 0.0)
            rank = rank + jnp.where(jb > i, ge, gt)
        sel_t = jnp.where(rank < n_top, 1.0, 0.0)
        sel_t = jnp.concatenate([sel_t, jnp.zeros((LANES - nsel, tq), F32)], axis=0)
        sel_bf = sel_t.T.astype(BF16)

        def sel_mask(k0, blk, sel_bf=sel_bf):
            jj = lax.broadcasted_iota(jnp.int32, (LANES, blk), 0)
            kk = lax.broadcasted_iota(jnp.int32, (LANES, blk), 1)
            onehot = jnp.where(jj == (k0 + kk) // SEL_LEN, 1.0, 0.0).astype(BF16)
            chosen = jnp.dot(sel_bf, onehot, preferred_element_type=F32)
            key, t = pos_grids(k0, blk)
            d = t - key
            return (d >= 0) & (chosen > 0.5), (-d).astype(F32)

        o_slc = flash(qs4, 2 + g, 0, last_blk, tk, sel_mask, slopes4)
        win_lo = jnp.maximum((t0 - (NSA_WINDOW - 1)) // tk, 0)
        o_win = flash(qs4, 4 + g, win_lo, last_blk, tk, band_mask(NSA_WINDOW), slopes4)

        y4 = []
        for r in range(N_REP):
            c = (g * N_REP + r) * 3
            y4.append(gate_col(c) * o_cmp[r] + gate_col(c + 1) * o_slc[r] + gate_col(c + 2) * o_win[r])
        ya_ref[0, :, g * N_REP * HEAD_DIM:(g + 1) * N_REP * HEAD_DIM] = pack_heads(y4).astype(BF16)

        qb4 = [head_q(qs_ref, g * N_REP + r) for r in range(N_REP)]
        sink4 = [sinks_ref[g * N_REP + r] for r in range(N_REP)]
        swa_lo = jnp.maximum((t0 - (SWA_WINDOW - 1)) // tk_swa, 0)
        swa_hi = (t0 + tq - 1) // tk_swa + 1
        o_swa = flash(qb4, 6 + g, swa_lo, swa_hi, tk_swa, band_mask(SWA_WINDOW), slopes4, sink4)
        yb_ref[0, :, g * N_REP * HEAD_DIM:(g + 1) * N_REP * HEAD_DIM] = pack_heads(o_swa).astype(BF16)


def _attn_call(sinks, qn, qs, gn, kv, kvc, *, tq=256, tk=256, tk_swa=128):
    bsz, s, _ = qn.shape
    ncmp = kvc.shape[2]
    qspec = pl.BlockSpec((1, tq, Q_COLS), lambda b, i: (b, i, 0))
    return pl.pallas_call(
        functools.partial(_attn_kernel, seq=s, tq=tq, tk=tk, tk_swa=tk_swa),
        grid=(bsz, s // tq),
        in_specs=[pl.BlockSpec(memory_space=pltpu.SMEM),
                  qspec, qspec,
                  pl.BlockSpec((1, tq, LANES), lambda b, i: (b, i, 0)),
                  pl.BlockSpec((1, N_KVSLOT, s, LANES), lambda b, i: (b, 0, 0, 0)),
                  pl.BlockSpec((1, N_KV, ncmp, LANES), lambda b, i: (b, 0, 0, 0))],
        out_specs=[qspec, qspec],
        out_shape=[jax.ShapeDtypeStruct((bsz, s, Q_COLS), BF16),
                   jax.ShapeDtypeStruct((bsz, s, Q_COLS), BF16)],
        scratch_shapes=[pltpu.VMEM((N_REP, tq, LANES), F32),
                        pltpu.VMEM((N_REP, tq, LANES), F32),
                        pltpu.VMEM((N_REP, tq, LANES), F32)],
        compiler_params=_params(("parallel", "arbitrary")),
        name="hybrid_attention",
    )(sinks, qn, qs, gn, kv, kvc)


def _merge_kernel(x_ref, sh_ref, sc_ref, gt_ref, g_ref, ya_ref, yb_ref, wgm_ref, wa_ref, wb_ref, wo_ref, o_ref):
    x = x_ref[0]
    d = x.shape[-1]
    u = _modulated_norm(x, g_ref[...], sh_ref[0], sc_ref[0]).astype(BF16)
    up_a = jnp.dot(ya_ref[0], wa_ref[...], preferred_element_type=F32)
    gate_a = jnp.dot(u, wgm_ref[:, :d], preferred_element_type=F32)
    merged = jax.nn.sigmoid(gate_a) * up_a
    up_b = jnp.dot(yb_ref[0], wb_ref[...], preferred_element_type=F32)
    gate_b = jnp.dot(u, wgm_ref[:, d:], preferred_element_type=F32)
    merged = merged + jax.nn.sigmoid(gate_b) * up_b
    y = jnp.dot(merged.astype(BF16), wo_ref[...], preferred_element_type=F32)
    o_ref[0] = x + gt_ref[0] * y


def _merge_call(h, shift, scale, gate, g, ya, yb, wgm, wa, wb, wo, *, tm=512):
    bsz, s, d = h.shape
    vec = pl.BlockSpec((1, 1, d), lambda b, i: (b, 0, 0))
    tok = pl.BlockSpec((1, tm, d), lambda b, i: (b, i, 0))
    ysp = pl.BlockSpec((1, tm, Q_COLS), lambda b, i: (b, i, 0))
    return pl.pallas_call(
        _merge_kernel,
        grid=(bsz, s // tm),
        in_specs=[tok, vec, vec, vec, _const_spec((1, d)), ysp, ysp,
                  _const_spec(wgm.shape), _const_spec(wa.shape), _const_spec(wb.shape), _const_spec(wo.shape)],
        out_specs=tok,
        out_shape=jax.ShapeDtypeStruct((bsz, s, d), F32),
        compiler_params=_params(("parallel", "parallel")),
        name="mixer_merge",
    )(h, shift, scale, gate, g.reshape(1, d), ya, yb, wgm, wa, wb, wo)


def _proj_column_order():
    kvw = 2 * N_KV * HEAD_DIM
    off_qn = 0
    off_c = off_qn + Q_COLS
    off_s = off_c + kvw
    off_w = off_s + kvw
    off_gn = off_w + kvw
    off_qs = off_gn + 3 * N_HEADS
    off_b = off_qs + Q_COLS
    off_gm = off_b + kvw
    cols = list(range(off_qn, off_qn + Q_COLS)) + list(range(off_qs, off_qs + Q_COLS))
    for off in (off_c, off_s, off_w, off_b):
        for g in range(N_KV):
            cols += list(range(off + g * HEAD_DIM, off + (g + 1) * HEAD_DIM))
            cols += list(range(off + (N_KV + g) * HEAD_DIM, off + (N_KV + g + 1) * HEAD_DIM))
    cols += list(range(off_gn, off_gn + 3 * N_HEADS))
    return np.asarray(cols, np.int32), off_gm


def _compress_weights(pos_k, w1_k, w2_k, pos_v, w1_v, w2_v):
    half = CMP_LEN // 2
    zk = jnp.zeros((half, HEAD_DIM, CMP_HIDDEN), F32)

    def w1_half(sl):
        wk = jnp.concatenate([w1_k[sl], zk], axis=-1)
        wv = jnp.concatenate([zk, w1_v[sl]], axis=-1)
        return jnp.concatenate([wk, wv], axis=1).reshape(half * 2 * HEAD_DIM, 2 * CMP_HIDDEN).astype(BF16)

    def pos_half(sl):
        p = jnp.concatenate([pos_k[sl], pos_v[sl]], axis=1).reshape(1, half * 2 * HEAD_DIM)
        return jnp.broadcast_to(p, (8, p.shape[1])).astype(BF16)

    z2 = jnp.zeros((CMP_HIDDEN, HEAD_DIM), F32)
    w2 = jnp.concatenate([jnp.concatenate([w2_k, z2], axis=1),
                          jnp.concatenate([z2, w2_v], axis=1)], axis=0).astype(BF16)
    lo, hi = slice(0, half), slice(half, CMP_LEN)
    return pos_half(lo), pos_half(hi), w1_half(lo), w1_half(hi), w2


def kernel(x, c, w_ada, b_ada, g_ffn1, w1_gate, w1_up, w1_down, g_mix, w_in, cmp_pos_k, cmp_w1_k, cmp_w2_k,
           cmp_pos_v, cmp_w1_v, cmp_w2_v, sinks, w_up_a, w_up_b, w_out, g_ffn2, w2_gate, w2_up, w2_down, g_final):
    bsz, seq, d = x.shape
    depth = w_ada.shape[0]
    cols, off_gm = _proj_column_order()
    h = x
    for l in range(depth):
        mod = _ada_call(c, w_ada[l], b_ada[l])
        sh1, sc1, gt1, sh2, sc2, gt2, sh3, sc3, gt3 = [m.reshape(bsz, 1, d) for m in jnp.split(mod, 9, axis=-1)]
        last = l == depth - 1

        h = _ffn_call(h, sh1, sc1, gt1, g_ffn1[l],
                      w1_gate[l].astype(BF16), w1_up[l].astype(BF16), w1_down[l].astype(BF16))

        w_proj = jnp.concatenate(
            [w_in[l][:, cols], jnp.zeros((d, PROJ_COLS - cols.shape[0]), F32)], axis=1).astype(BF16)
        qn, qs, kv, gn = _proj_call(h, sh2, sc2, g_mix[l], w_proj)

        pa, pb, w1a, w1b, w2c = _compress_weights(cmp_pos_k[l], cmp_w1_k[l], cmp_w2_k[l],
                                                  cmp_pos_v[l], cmp_w1_v[l], cmp_w2_v[l])
        nchunk = seq // CMP_STRIDE
        kv_chunks = kv[:, :N_KV].reshape(bsz, N_KV, nchunk, CMP_STRIDE * LANES)
        kvc = _cmp_call(kv_chunks, pa, pb, w1a, w1b, w2c)

        ya, yb = _attn_call(sinks[l].reshape(-1), qn, qs, gn, kv, kvc)

        h = _merge_call(h, sh2, sc2, gt2, g_mix[l], ya, yb,
                        w_in[l][:, off_gm:].astype(BF16), w_up_a[l].astype(BF16),
                        w_up_b[l].astype(BF16), w_out[l].astype(BF16))

        h = _ffn_call(h, sh3, sc3, gt3, g_ffn2[l],
                      w2_gate[l].astype(BF16), w2_up[l].astype(BF16), w2_down[l].astype(BF16),
                      g_final if last else None)
    if depth == 0:
        raise ValueError("depth must be positive")
    return h
```

```python
import functools

import numpy as np
import jax
import jax.numpy as jnp
from jax import lax
from jax.experimental import pallas as pl
from jax.experimental.pallas import tpu as pltpu

F32 = jnp.float32
BF16 = jnp.bfloat16

HEAD_DIM = 64
N_HEADS = 8
N_KV = 2
N_REP = N_HEADS // N_KV
CMP_LEN = 32
CMP_STRIDE = 16
CMP_HIDDEN = 256
SEL_LEN = 64
SEL_TOPN = 8
NSA_WINDOW = 512
SWA_WINDOW = 128
FFN_RES = 0.5
RMS_EPS = 1e-6
NEG_INF = -1e30
SEL_BONUS = 1e4
ATTN_SCALE = HEAD_DIM ** -0.5

LANES = 128
VMEM_LIMIT = 56 * 1024 * 1024

SLOPES = [2.0 ** (-8.0 * (h + 1) / N_HEADS) for h in range(N_HEADS)]


def _const_spec(shape):
    n = len(shape)
    return pl.BlockSpec(shape, lambda *_: (0,) * n, pipeline_mode=pl.Buffered(1))


def _params(sem):
    return pltpu.CompilerParams(dimension_semantics=sem, vmem_limit_bytes=VMEM_LIMIT)


def _modulated_norm(x, g, shift, scale):
    ms = jnp.mean(x * x, axis=-1, keepdims=True)
    y = x * lax.rsqrt(ms + RMS_EPS)
    return (y * g) * (1.0 + scale) + shift


def _split3(a):
    hi = a.astype(BF16)
    r1 = a - hi.astype(F32)
    mid = r1.astype(BF16)
    lo = (r1 - mid.astype(F32)).astype(BF16)
    return hi, mid, lo


def _ada_kernel(c_ref, w_ref, b_ref, o_ref):
    c = c_ref[...]
    a = c * jax.nn.sigmoid(c)
    a_hi = a.astype(BF16)
    a_lo = (a - a_hi.astype(F32)).astype(BF16)
    w = w_ref[...]
    w_hi = w.astype(BF16)
    w_lo = (w - w_hi.astype(F32)).astype(BF16)
    acc = jnp.dot(a_hi, w_hi, preferred_element_type=F32)
    acc += jnp.dot(a_hi, w_lo, preferred_element_type=F32)
    acc += jnp.dot(a_lo, w_hi, preferred_element_type=F32)
    o_ref[...] = acc + b_ref[...]


def _ada_call(c, w, b):
    bsz, d = c.shape
    n = w.shape[1]
    tn = 1024
    return pl.pallas_call(
        _ada_kernel,
        grid=(n // tn,),
        in_specs=[pl.BlockSpec((bsz, d), lambda j: (0, 0)),
                  pl.BlockSpec((d, tn), lambda j: (0, j)),
                  pl.BlockSpec((1, tn), lambda j: (0, j))],
        out_specs=pl.BlockSpec((bsz, tn), lambda j: (0, j)),
        out_shape=jax.ShapeDtypeStruct((bsz, n), F32),
        compiler_params=_params(("parallel",)),
        name="adaln",
    )(c, w, b.reshape(1, n))


def _ffn_kernel(x_ref, sh_ref, sc_ref, gt_ref, g_ref, wg_ref, wu_ref, wd_ref, *rest, tf, final):
    o_ref = rest[-1]
    x = x_ref[0]
    u = _modulated_norm(x, g_ref[...], sh_ref[0], sc_ref[0]).astype(BF16)
    dff = wg_ref.shape[1]
    acc = None
    for c in range(dff // tf):
        cols = slice(c * tf, (c + 1) * tf)
        gate = jnp.dot(u, wg_ref[:, cols], preferred_element_type=F32)
        up = jnp.dot(u, wu_ref[:, cols], preferred_element_type=F32)
        act = (gate * jax.nn.sigmoid(gate) * up).astype(BF16)
        part = jnp.dot(act, wd_ref[cols, :], preferred_element_type=F32)
        acc = part if acc is None else acc + part
    h = x + (FFN_RES * gt_ref[0]) * acc
    if final:
        gfin_ref = rest[0]
        ms = jnp.mean(h * h, axis=-1, keepdims=True)
        h = (h * lax.rsqrt(ms + RMS_EPS)) * gfin_ref[...]
    o_ref[0] = h


def _ffn_call(h, shift, scale, gate, g, wg, wu, wd, g_final=None, *, tm=512, tf=256):
    bsz, s, d = h.shape
    dff = wg.shape[1]
    final = g_final is not None
    vec = pl.BlockSpec((1, 1, d), lambda b, i: (b, 0, 0))
    in_specs = [pl.BlockSpec((1, tm, d), lambda b, i: (b, i, 0)), vec, vec, vec,
                _const_spec((1, d)), _const_spec((d, dff)), _const_spec((d, dff)), _const_spec((dff, d))]
    args = [h, shift, scale, gate, g.reshape(1, d), wg, wu, wd]
    if final:
        in_specs.append(_const_spec((1, d)))
        args.append(g_final.reshape(1, d))
    return pl.pallas_call(
        functools.partial(_ffn_kernel, tf=tf, final=final),
        grid=(bsz, s // tm),
        in_specs=in_specs,
        out_specs=pl.BlockSpec((1, tm, d), lambda b, i: (b, i, 0)),
        out_shape=jax.ShapeDtypeStruct((bsz, s, d), F32),
        compiler_params=_params(("parallel", "parallel")),
        name="ffn_final" if final else "ffn",
    )(*args)


N_KVSLOT = 4 * N_KV
Q_COLS = N_HEADS * HEAD_DIM
PROJ_COLS = 2 * Q_COLS + N_KVSLOT * LANES + LANES


def _proj_kernel(x_ref, sh_ref, sc_ref, g_ref, w_ref, qn_ref, qs_ref, kv_ref, gn_ref):
    u = _modulated_norm(x_ref[0], g_ref[...], sh_ref[0], sc_ref[0]).astype(BF16)
    proj = jnp.dot(u, w_ref[...], preferred_element_type=F32)
    qn_ref[0] = (proj[:, 0:Q_COLS] * ATTN_SCALE).astype(BF16)
    qs_ref[0] = (proj[:, Q_COLS:2 * Q_COLS] * ATTN_SCALE).astype(BF16)
    base = 2 * Q_COLS
    for i in range(N_KVSLOT):
        kv_ref[0, i] = proj[:, base + i * LANES: base + (i + 1) * LANES].astype(BF16)
    gn_ref[0] = proj[:, base + N_KVSLOT * LANES:]


def _proj_call(h, shift, scale, g, w, *, tm=512):
    bsz, s, d = h.shape
    vec = pl.BlockSpec((1, 1, d), lambda b, i: (b, 0, 0))
    return pl.pallas_call(
        _proj_kernel,
        grid=(bsz, s // tm),
        in_specs=[pl.BlockSpec((1, tm, d), lambda b, i: (b, i, 0)), vec, vec,
                  _const_spec((1, d)), _const_spec((d, PROJ_COLS))],
        out_specs=[pl.BlockSpec((1, tm, Q_COLS), lambda b, i: (b, i, 0)),
                   pl.BlockSpec((1, tm, Q_COLS), lambda b, i: (b, i, 0)),
                   pl.BlockSpec((1, N_KVSLOT, tm, LANES), lambda b, i: (b, 0, i, 0)),
                   pl.BlockSpec((1, tm, LANES), lambda b, i: (b, i, 0))],
        out_shape=[jax.ShapeDtypeStruct((bsz, s, Q_COLS), BF16),
                   jax.ShapeDtypeStruct((bsz, s, Q_COLS), BF16),
                   jax.ShapeDtypeStruct((bsz, N_KVSLOT, s, LANES), BF16),
                   jax.ShapeDtypeStruct((bsz, s, LANES), F32)],
        compiler_params=_params(("parallel", "parallel")),
        name="mixer_proj",
    )(h, shift, scale, g.reshape(1, d), w)


def _cmp_kernel(a_ref, pa_ref, pb_ref, w1a_ref, w1b_ref, w2_ref, o_ref):
    a = a_ref[0, 0]
    first = jnp.dot(a, w1a_ref[...], preferred_element_type=F32)
    second = jnp.dot(a, w1b_ref[...], preferred_element_type=F32)
    bias = (jnp.dot(pa_ref[...], w1a_ref[...], preferred_element_type=F32)
            + jnp.dot(pb_ref[...], w1b_ref[...], preferred_element_type=F32))[0:1]
    n = a.shape[0]
    hid = first + pltpu.roll(second, n - 1, axis=0) + bias
    hid = jax.nn.gelu(hid)
    o_ref[0, 0] = jnp.dot(hid.astype(BF16), w2_ref[...], preferred_element_type=F32).astype(BF16)


def _cmp_call(kv_chunks, pa, pb, w1a, w1b, w2):
    bsz, _, nchunk, width = kv_chunks.shape
    return pl.pallas_call(
        _cmp_kernel,
        grid=(bsz, N_KV),
        in_specs=[pl.BlockSpec((1, 1, nchunk, width), lambda b, g: (b, g, 0, 0)),
                  _const_spec(pa.shape), _const_spec(pb.shape),
                  _const_spec(w1a.shape), _const_spec(w1b.shape), _const_spec(w2.shape)],
        out_specs=pl.BlockSpec((1, 1, nchunk, LANES), lambda b, g: (b, g, 0, 0)),
        out_shape=jax.ShapeDtypeStruct((bsz, N_KV, nchunk, LANES), BF16),
        compiler_params=_params(("parallel", "parallel")),
        name="nsa_compress",
    )(kv_chunks, pa, pb, w1a, w1b, w2)


POS_HI, POS_LO, POS_ONE = 96, 97, 98
MAX_SEL_BLOCKS = POS_HI


def _position_tables(seq):
    key = np.arange(seq)
    kpos = np.zeros((seq, LANES), np.float32)
    kpos[key, key // SEL_LEN] = 1.0
    kpos[:, POS_HI] = (key // SEL_LEN) * SEL_LEN
    kpos[:, POS_LO] = key % SEL_LEN
    kpos[:, POS_ONE] = 1.0
    ncmp = seq // CMP_STRIDE
    cpos = np.zeros((ncmp, LANES), np.float32)
    cpos[:, POS_HI] = np.arange(ncmp) * CMP_STRIDE
    cpos[:, POS_LO] = CMP_LEN - 1
    cpos[:, POS_ONE] = 1.0
    return jnp.asarray(kpos, BF16), jnp.asarray(cpos, BF16)


def _attn_kernel(sinks_ref, qn_ref, qs_ref, gn_ref, kv_ref, kvc_ref, kpos_ref, cpos_ref, ya_ref, yb_ref,
                 s_scr, m_scr, acc_scr, *, seq, tq):
    qi = pl.program_id(1)
    t0 = qi * tq
    rows = N_REP * tq
    nsel = seq // SEL_LEN
    ncmp = kvc_ref.shape[2]
    n_top = min(SEL_TOPN, nsel)
    lane = lax.broadcasted_iota(jnp.int32, (tq, LANES), 1)
    t0f = t0.astype(F32)

    def nt_dot(a, b):
        return lax.dot_general(a, b, (((1,), (1,)), ((), ())), preferred_element_type=F32)

    def head_q(q_ref, h):
        pair = q_ref[0, :, (h // 2) * LANES:(h // 2 + 1) * LANES].astype(F32)
        if h % 2:
            pair = pltpu.roll(pair, HEAD_DIM, axis=1)
        return jnp.where(lane < HEAD_DIM, pair, 0.0).astype(BF16)

    def stacked_q(q_ref, g, sel=None):
        parts = []
        for r in range(N_REP):
            slope = SLOPES[g * N_REP + r]
            ext = jnp.where((lane == POS_HI) | (lane == POS_LO), slope,
                            jnp.where(lane == POS_ONE, -slope * t0f, 0.0))
            if sel is not None:
                ext = ext + jnp.where(lane < nsel, (sel - 1.0) * (-NEG_INF), 0.0)
            parts.append(jnp.concatenate([head_q(q_ref, g * N_REP + r), ext.astype(BF16)], axis=1))
        return jnp.concatenate(parts, axis=0)

    def masked(s, mask):
        return jnp.concatenate([jnp.where(mask, s[r * tq:(r + 1) * tq], NEG_INF) for r in range(N_REP)], axis=0)

    def pv_rhs(kv):
        return jnp.concatenate([kv, jnp.ones_like(kv)], axis=1)

    def band_branch(q4, slot, window, sink_col=None):
        span = tq + -(-window // LANES) * LANES
        k_start = pl.multiple_of(jnp.maximum(t0 + tq - span, 0), LANES)
        kv = kv_ref[0, slot, pl.ds(k_start, span), :]
        s = nt_dot(q4, jnp.concatenate([kv, kpos_ref[pl.ds(k_start, span), :]], axis=1))
        d = ((t0 - k_start) + lax.broadcasted_iota(jnp.int32, (tq, span), 0)
             - lax.broadcasted_iota(jnp.int32, (tq, span), 1))
        s = masked(s, (d >= 0) & (d < window))
        m = jnp.max(s, axis=1, keepdims=True)
        if sink_col is not None:
            m = jnp.maximum(m, sink_col)
        p = jnp.exp(s - m).astype(BF16)
        o = jnp.dot(p, pv_rhs(kv), preferred_element_type=F32)
        l = o[:, LANES:]
        if sink_col is not None:
            l = l + jnp.exp(sink_col - m)
        return o[:, :LANES] / l

    def pack_heads(o4):
        pairs = []
        for p in range(N_REP // 2):
            left = pltpu.roll(o4[2 * p], HEAD_DIM, axis=1)
            pairs.append(jnp.where(lane < HEAD_DIM, left, o4[2 * p + 1]))
        return jnp.concatenate(pairs, axis=1)

    gsig = jax.nn.sigmoid(gn_ref[0])

    def gate_col(c):
        return jnp.sum(jnp.where(lane == c, gsig, 0.0), axis=1, keepdims=True)

    for g in range(N_KV):
        q4 = stacked_q(qn_ref, g)

        kvc = kvc_ref[0, g]
        s = nt_dot(q4, jnp.concatenate([kvc, cpos_ref[...]], axis=1))
        tc = t0 + lax.broadcasted_iota(jnp.int32, (tq, ncmp), 0)
        cend = lax.broadcasted_iota(jnp.int32, (tq, ncmp), 1) * CMP_STRIDE + (CMP_LEN - 1)
        s = masked(s, tc >= cend)
        e = jnp.exp(s - jnp.max(s, axis=1, keepdims=True))
        p = e / jnp.sum(e, axis=1, keepdims=True)
        row_ok = tc >= (CMP_LEN - 1)
        p = jnp.concatenate([jnp.where(row_ok, p[r * tq:(r + 1) * tq], 0.0) for r in range(N_REP)], axis=0)
        o_cmp = jnp.dot(p.astype(BF16), kvc, preferred_element_type=F32)
        psum = p[0:tq]
        for r in range(1, N_REP):
            psum = psum + p[r * tq:(r + 1) * tq]

        jn = lax.broadcasted_iota(jnp.int32, (nsel, ncmp), 0) * SEL_LEN
        cn = lax.broadcasted_iota(jnp.int32, (nsel, ncmp), 1) * CMP_STRIDE
        ov_t = jnp.where((cn < jn + SEL_LEN) & (cn + CMP_LEN > jn), 1.0, 0.0).astype(BF16)
        imp_t = sum(nt_dot(ov_t, part) for part in _split3(psum))
        jb = lax.broadcasted_iota(jnp.int32, (nsel, tq), 0)
        cur = (t0 + lax.broadcasted_iota(jnp.int32, (nsel, tq), 1)) // SEL_LEN
        forced = (jb == 0) | (jb == cur) | (jb == cur - 1)
        score = jnp.where(forced, SEL_BONUS, jnp.where(jb <= cur, imp_t, -1.0))
        rank = jnp.zeros((nsel, tq), F32)
        for i in range(nsel):
            row = score[i:i + 1, :]
            gt = jnp.where(row > score, 1.0, 0.0)
            ge = jnp.where(row >= score, 1.0, 0.0)
            rank = rank + jnp.where(jb > i, ge, gt)
        sel_t = jnp.where(rank < n_top, 1.0, 0.0)
        sel_t = jnp.concatenate([sel_t, jnp.zeros((LANES - nsel, tq), F32)], axis=0)
        sel = sel_t.T

        q4s = stacked_q(qn_ref, g, sel)
        slot = 2 + g

        def scores(kb, q4s=q4s, slot=slot):
            k0 = pl.multiple_of(kb * tq, tq)
            kaug = jnp.concatenate([kv_ref[0, slot, pl.ds(k0, tq), :], kpos_ref[pl.ds(k0, tq), :]], axis=1)
            return nt_dot(q4s, kaug)

        def fold_max(s):
            mp = m_scr[...]
            for j in range(tq // LANES):
                mp = jnp.maximum(mp, s[:, j * LANES:(j + 1) * LANES])
            m_scr[...] = mp

        m_scr[...] = jnp.full((rows, LANES), NEG_INF, F32)

        def pass1(kb, carry):
            s = scores(kb)
            s_scr[kb] = s
            fold_max(s)
            return carry

        lax.fori_loop(0, qi, pass1, 0)
        causal = (lax.broadcasted_iota(jnp.int32, (tq, tq), 0) >= lax.broadcasted_iota(jnp.int32, (tq, tq), 1))
        s = masked(scores(qi), causal)
        s_scr[qi] = s
        fold_max(s)
        m_scr[...] = jnp.broadcast_to(jnp.max(m_scr[...], axis=1, keepdims=True), (rows, LANES))
        acc_scr[...] = jnp.zeros((rows, 2 * LANES), F32)

        def pass2(kb, carry, slot=slot):
            k0 = pl.multiple_of(kb * tq, tq)
            p = jnp.exp(s_scr[kb] - pltpu.repeat(m_scr[...], tq // LANES, 1)).astype(BF16)
            acc_scr[...] += jnp.dot(p, pv_rhs(kv_ref[0, slot, pl.ds(k0, tq), :]), preferred_element_type=F32)
            return carry

        lax.fori_loop(0, qi + 1, pass2, 0)
        o_slc = acc_scr[:, :LANES] / acc_scr[:, LANES:]

        o_win = band_branch(q4, 4 + g, NSA_WINDOW)

        y4 = []
        for r in range(N_REP):
            c = (g * N_REP + r) * 3
            seg = slice(r * tq, (r + 1) * tq)
            y4.append(gate_col(c) * o_cmp[seg] + gate_col(c + 1) * o_slc[seg] + gate_col(c + 2) * o_win[seg])
        ya_ref[0, :, g * N_REP * HEAD_DIM:(g + 1) * N_REP * HEAD_DIM] = pack_heads(y4).astype(BF16)

        trow = lax.broadcasted_iota(jnp.int32, (tq, 1), 0).astype(F32)
        sink_col = jnp.concatenate(
            [sinks_ref[g * N_REP + r] + SLOPES[g * N_REP + r] * trow for r in range(N_REP)], axis=0)
        o_swa = band_branch(stacked_q(qs_ref, g), 6 + g, SWA_WINDOW, sink_col)
        yb_ref[0, :, g * N_REP * HEAD_DIM:(g + 1) * N_REP * HEAD_DIM] = pack_heads(
            [o_swa[r * tq:(r + 1) * tq] for r in range(N_REP)]).astype(BF16)


def _attn_call(sinks, qn, qs, gn, kv, kvc, *, tq=256):
    bsz, s, _ = qn.shape
    ncmp = kvc.shape[2]
    assert s % tq == 0 and s // SEL_LEN <= MAX_SEL_BLOCKS and s >= tq + NSA_WINDOW
    kpos, cpos = _position_tables(s)
    qspec = pl.BlockSpec((1, tq, Q_COLS), lambda b, i: (b, i, 0))
    rows = N_REP * tq
    return pl.pallas_call(
        functools.partial(_attn_kernel, seq=s, tq=tq),
        grid=(bsz, s // tq),
        in_specs=[pl.BlockSpec(memory_space=pltpu.SMEM),
                  qspec, qspec,
                  pl.BlockSpec((1, tq, LANES), lambda b, i: (b, i, 0)),
                  pl.BlockSpec((1, N_KVSLOT, s, LANES), lambda b, i: (b, 0, 0, 0)),
                  pl.BlockSpec((1, N_KV, ncmp, LANES), lambda b, i: (b, 0, 0, 0)),
                  _const_spec((s, LANES)), _const_spec((ncmp, LANES))],
        out_specs=[qspec, qspec],
        out_shape=[jax.ShapeDtypeStruct((bsz, s, Q_COLS), BF16),
                   jax.ShapeDtypeStruct((bsz, s, Q_COLS), BF16)],
        scratch_shapes=[pltpu.VMEM((s // tq, rows, tq), F32),
                        pltpu.VMEM((rows, LANES), F32),
                        pltpu.VMEM((rows, 2 * LANES), F32)],
        compiler_params=_params(("parallel", "arbitrary")),
        name="hybrid_attention",
    )(sinks, qn, qs, gn, kv, kvc, kpos, cpos)


def _merge_kernel(x_ref, sh_ref, sc_ref, gt_ref, g_ref, ya_ref, yb_ref, wgm_ref, wa_ref, wb_ref, wo_ref, o_ref):
    x = x_ref[0]
    d = x.shape[-1]
    u = _modulated_norm(x, g_ref[...], sh_ref[0], sc_ref[0]).astype(BF16)
    up_a = jnp.dot(ya_ref[0], wa_ref[...], preferred_element_type=F32)
    gate_a = jnp.dot(u, wgm_ref[:, :d], preferred_element_type=F32)
    merged = jax.nn.sigmoid(gate_a) * up_a
    up_b = jnp.dot(yb_ref[0], wb_ref[...], preferred_element_type=F32)
    gate_b = jnp.dot(u, wgm_ref[:, d:], preferred_element_type=F32)
    merged = merged + jax.nn.sigmoid(gate_b) * up_b
    y = jnp.dot(merged.astype(BF16), wo_ref[...], preferred_element_type=F32)
    o_ref[0] = x + gt_ref[0] * y


def _merge_call(h, shift, scale, gate, g, ya, yb, wgm, wa, wb, wo, *, tm=512):
    bsz, s, d = h.shape
    vec = pl.BlockSpec((1, 1, d), lambda b, i: (b, 0, 0))
    tok = pl.BlockSpec((1, tm, d), lambda b, i: (b, i, 0))
    ysp = pl.BlockSpec((1, tm, Q_COLS), lambda b, i: (b, i, 0))
    return pl.pallas_call(
        _merge_kernel,
        grid=(bsz, s // tm),
        in_specs=[tok, vec, vec, vec, _const_spec((1, d)), ysp, ysp,
                  _const_spec(wgm.shape), _const_spec(wa.shape), _const_spec(wb.shape), _const_spec(wo.shape)],
        out_specs=tok,
        out_shape=jax.ShapeDtypeStruct((bsz, s, d), F32),
        compiler_params=_params(("parallel", "parallel")),
        name="mixer_merge",
    )(h, shift, scale, gate, g.reshape(1, d), ya, yb, wgm, wa, wb, wo)


def _proj_column_order():
    kvw = 2 * N_KV * HEAD_DIM
    off_qn = 0
    off_c = off_qn + Q_COLS
    off_s = off_c + kvw
    off_w = off_s + kvw
    off_gn = off_w + kvw
    off_qs = off_gn + 3 * N_HEADS
    off_b = off_qs + Q_COLS
    off_gm = off_b + kvw
    cols = list(range(off_qn, off_qn + Q_COLS)) + list(range(off_qs, off_qs + Q_COLS))
    for off in (off_c, off_s, off_w, off_b):
        for g in range(N_KV):
            cols += list(range(off + g * HEAD_DIM, off + (g + 1) * HEAD_DIM))
            cols += list(range(off + (N_KV + g) * HEAD_DIM, off + (N_KV + g + 1) * HEAD_DIM))
    cols += list(range(off_gn, off_gn + 3 * N_HEADS))
    return np.asarray(cols, np.int32), off_gm


def _compress_weights(pos_k, w1_k, w2_k, pos_v, w1_v, w2_v):
    half = CMP_LEN // 2
    zk = jnp.zeros((half, HEAD_DIM, CMP_HIDDEN), F32)

    def w1_half(sl):
        wk = jnp.concatenate([w1_k[sl], zk], axis=-1)
        wv = jnp.concatenate([zk, w1_v[sl]], axis=-1)
        return jnp.concatenate([wk, wv], axis=1).reshape(half * 2 * HEAD_DIM, 2 * CMP_HIDDEN).astype(BF16)

    def pos_half(sl):
        p = jnp.concatenate([pos_k[sl], pos_v[sl]], axis=1).reshape(1, half * 2 * HEAD_DIM)
        return jnp.broadcast_to(p, (8, p.shape[1])).astype(BF16)

    z2 = jnp.zeros((CMP_HIDDEN, HEAD_DIM), F32)
    w2 = jnp.concatenate([jnp.concatenate([w2_k, z2], axis=1),
                          jnp.concatenate([z2, w2_v], axis=1)], axis=0).astype(BF16)
    lo, hi = slice(0, half), slice(half, CMP_LEN)
    return pos_half(lo), pos_half(hi), w1_half(lo), w1_half(hi), w2


def kernel(x, c, w_ada, b_ada, g_ffn1, w1_gate, w1_up, w1_down, g_mix, w_in, cmp_pos_k, cmp_w1_k, cmp_w2_k,
           cmp_pos_v, cmp_w1_v, cmp_w2_v, sinks, w_up_a, w_up_b, w_out, g_ffn2, w2_gate, w2_up, w2_down, g_final):
    bsz, seq, d = x.shape
    depth = w_ada.shape[0]
    cols, off_gm = _proj_column_order()
    h = x
    for l in range(depth):
        mod = _ada_call(c, w_ada[l], b_ada[l])
        sh1, sc1, gt1, sh2, sc2, gt2, sh3, sc3, gt3 = [m.reshape(bsz, 1, d) for m in jnp.split(mod, 9, axis=-1)]
        last = l == depth - 1

        h = _ffn_call(h, sh1, sc1, gt1, g_ffn1[l],
                      w1_gate[l].astype(BF16), w1_up[l].astype(BF16), w1_down[l].astype(BF16))

        w_proj = jnp.concatenate(
            [w_in[l][:, cols], jnp.zeros((d, PROJ_COLS - cols.shape[0]), F32)], axis=1).astype(BF16)
        qn, qs, kv, gn = _proj_call(h, sh2, sc2, g_mix[l], w_proj)

        pa, pb, w1a, w1b, w2c = _compress_weights(cmp_pos_k[l], cmp_w1_k[l], cmp_w2_k[l],
                                                  cmp_pos_v[l], cmp_w1_v[l], cmp_w2_v[l])
        nchunk = seq // CMP_STRIDE
        kv_chunks = kv[:, :N_KV].reshape(bsz, N_KV, nchunk, CMP_STRIDE * LANES)
        kvc = _cmp_call(kv_chunks, pa, pb, w1a, w1b, w2c)

        ya, yb = _attn_call(sinks[l].reshape(-1), qn, qs, gn, kv, kvc)

        h = _merge_call(h, sh2, sc2, gt2, g_mix[l], ya, yb,
                        w_in[l][:, off_gm:].astype(BF16), w_up_a[l].astype(BF16),
                        w_up_b[l].astype(BF16), w_out[l].astype(BF16))

        h = _ffn_call(h, sh3, sc3, gt3, g_ffn2[l],
                      w2_gate[l].astype(BF16), w2_up[l].astype(BF16), w2_down[l].astype(BF16),
                      g_final if last else None)
    return h
```

```python
import functools

import numpy as np
import jax
import jax.numpy as jnp
from jax import lax
from jax.experimental import pallas as pl
from jax.experimental.pallas import tpu as pltpu

F32 = jnp.float32
BF16 = jnp.bfloat16

HEAD_DIM = 64
N_HEADS = 8
N_KV = 2
N_REP = N_HEADS // N_KV
CMP_LEN = 32
CMP_STRIDE = 16
CMP_HIDDEN = 256
SEL_LEN = 64
SEL_TOPN = 8
NSA_WINDOW = 512
SWA_WINDOW = 128
FFN_RES = 0.5
RMS_EPS = 1e-6
NEG_INF = -1e30
SEL_BONUS = 1e4
ATTN_SCALE = HEAD_DIM ** -0.5

LANES = 128
VMEM_LIMIT = 56 * 1024 * 1024

SLOPES = [2.0 ** (-8.0 * (h + 1) / N_HEADS) for h in range(N_HEADS)]


def _const_spec(shape):
    n = len(shape)
    return pl.BlockSpec(shape, lambda *_: (0,) * n, pipeline_mode=pl.Buffered(1))


def _params(sem):
    return pltpu.CompilerParams(dimension_semantics=sem, vmem_limit_bytes=VMEM_LIMIT)


def _modulated_norm(x, g, shift, scale):
    ms = jnp.mean(x * x, axis=-1, keepdims=True)
    y = x * lax.rsqrt(ms + RMS_EPS)
    return (y * g) * (1.0 + scale) + shift


def _split3(a):
    hi = a.astype(BF16)
    r1 = a - hi.astype(F32)
    mid = r1.astype(BF16)
    lo = (r1 - mid.astype(F32)).astype(BF16)
    return hi, mid, lo


def _ada_kernel(c_ref, w_ref, b_ref, o_ref):
    c = c_ref[...]
    a = c * jax.nn.sigmoid(c)
    a_hi = a.astype(BF16)
    a_lo = (a - a_hi.astype(F32)).astype(BF16)
    w = w_ref[...]
    w_hi = w.astype(BF16)
    w_lo = (w - w_hi.astype(F32)).astype(BF16)
    acc = jnp.dot(a_hi, w_hi, preferred_element_type=F32)
    acc += jnp.dot(a_hi, w_lo, preferred_element_type=F32)
    acc += jnp.dot(a_lo, w_hi, preferred_element_type=F32)
    o_ref[...] = acc + b_ref[...]


def _ada_call(c, w, b):
    bsz, d = c.shape
    n = w.shape[1]
    tn = 1024
    return pl.pallas_call(
        _ada_kernel,
        grid=(n // tn,),
        in_specs=[pl.BlockSpec((bsz, d), lambda j: (0, 0)),
                  pl.BlockSpec((d, tn), lambda j: (0, j)),
                  pl.BlockSpec((1, tn), lambda j: (0, j))],
        out_specs=pl.BlockSpec((bsz, tn), lambda j: (0, j)),
        out_shape=jax.ShapeDtypeStruct((bsz, n), F32),
        compiler_params=_params(("parallel",)),
        name="adaln",
    )(c, w, b.reshape(1, n))


def _ffn_kernel(x_ref, sh_ref, sc_ref, gt_ref, g_ref, wg_ref, wu_ref, wd_ref, *rest, tf, final):
    o_ref = rest[-1]
    x = x_ref[0]
    u = _modulated_norm(x, g_ref[...], sh_ref[0], sc_ref[0]).astype(BF16)
    dff = wg_ref.shape[1]
    acc = None
    for c in range(dff // tf):
        cols = slice(c * tf, (c + 1) * tf)
        gate = jnp.dot(u, wg_ref[:, cols], preferred_element_type=F32)
        up = jnp.dot(u, wu_ref[:, cols], preferred_element_type=F32)
        act = (gate * jax.nn.sigmoid(gate) * up).astype(BF16)
        part = jnp.dot(act, wd_ref[cols, :], preferred_element_type=F32)
        acc = part if acc is None else acc + part
    h = x + (FFN_RES * gt_ref[0]) * acc
    if final:
        gfin_ref = rest[0]
        ms = jnp.mean(h * h, axis=-1, keepdims=True)
        h = (h * lax.rsqrt(ms + RMS_EPS)) * gfin_ref[...]
    o_ref[0] = h


def _ffn_call(h, shift, scale, gate, g, wg, wu, wd, g_final=None, *, tm=512, tf=256):
    bsz, s, d = h.shape
    dff = wg.shape[1]
    final = g_final is not None
    vec = pl.BlockSpec((1, 1, d), lambda b, i: (b, 0, 0))
    in_specs = [pl.BlockSpec((1, tm, d), lambda b, i: (b, i, 0)), vec, vec, vec,
                _const_spec((1, d)), _const_spec((d, dff)), _const_spec((d, dff)), _const_spec((dff, d))]
    args = [h, shift, scale, gate, g.reshape(1, d), wg, wu, wd]
    if final:
        in_specs.append(_const_spec((1, d)))
        args.append(g_final.reshape(1, d))
    return pl.pallas_call(
        functools.partial(_ffn_kernel, tf=tf, final=final),
        grid=(bsz, s // tm),
        in_specs=in_specs,
        out_specs=pl.BlockSpec((1, tm, d), lambda b, i: (b, i, 0)),
        out_shape=jax.ShapeDtypeStruct((bsz, s, d), F32),
        compiler_params=_params(("parallel", "parallel")),
        name="ffn_final" if final else "ffn",
    )(*args)


N_KPAIR = 3
Q_COLS = N_HEADS * HEAD_DIM
NN_COLS = (N_KPAIR + N_KV) * LANES
N_VSLOT = N_KPAIR * N_KV
GATE_ROWS = 32
VT_ROWS = N_VSLOT * HEAD_DIM
T_ROWS = 2 * Q_COLS + VT_ROWS + GATE_ROWS
KEY_CHUNK = LANES


def _proj_kernel(x_ref, sh_ref, sc_ref, g_ref, wn_ref, wt_ref, kp_ref, kvc_ref, qn_ref, qs_ref, vt_ref, gn_ref):
    u = _modulated_norm(x_ref[0], g_ref[...], sh_ref[0], sc_ref[0]).astype(BF16)
    tm = u.shape[0]
    nn = jnp.dot(u, wn_ref[...], preferred_element_type=F32)
    for i in range(N_KPAIR):
        kp_ref[0, i] = nn[:, i * LANES:(i + 1) * LANES].astype(BF16)
    for i in range(N_KV):
        kvc_ref[0, i] = nn[:, (N_KPAIR + i) * LANES:(N_KPAIR + i + 1) * LANES].astype(BF16)
    tt = lax.dot_general(wt_ref[...], u, (((1,), (1,)), ((), ())), preferred_element_type=F32)
    qn_ref[0] = (tt[0:Q_COLS] * ATTN_SCALE).astype(BF16)
    qs_ref[0] = (tt[Q_COLS:2 * Q_COLS] * ATTN_SCALE).astype(BF16)
    base = 2 * Q_COLS
    for s in range(N_VSLOT):
        for c in range(tm // KEY_CHUNK):
            vt_ref[0, s, c] = tt[base + s * HEAD_DIM: base + (s + 1) * HEAD_DIM,
                                 c * KEY_CHUNK:(c + 1) * KEY_CHUNK].astype(BF16)
    gn_ref[0] = tt[base + VT_ROWS:]


def _proj_call(h, shift, scale, g, wn, wt, *, tm=512):
    bsz, s, d = h.shape
    vec = pl.BlockSpec((1, 1, d), lambda b, i: (b, 0, 0))
    nck = tm // KEY_CHUNK
    return pl.pallas_call(
        _proj_kernel,
        grid=(bsz, s // tm),
        in_specs=[pl.BlockSpec((1, tm, d), lambda b, i: (b, i, 0)), vec, vec,
                  _const_spec((1, d)), _const_spec((d, NN_COLS)), _const_spec((T_ROWS, d))],
        out_specs=[pl.BlockSpec((1, N_KPAIR, tm, LANES), lambda b, i: (b, 0, i, 0)),
                   pl.BlockSpec((1, N_KV, tm, LANES), lambda b, i: (b, 0, i, 0)),
                   pl.BlockSpec((1, Q_COLS, tm), lambda b, i: (b, 0, i)),
                   pl.BlockSpec((1, Q_COLS, tm), lambda b, i: (b, 0, i)),
                   pl.BlockSpec((1, N_VSLOT, nck, HEAD_DIM, KEY_CHUNK), lambda b, i: (b, 0, i, 0, 0)),
                   pl.BlockSpec((1, GATE_ROWS, tm), lambda b, i: (b, 0, i))],
        out_shape=[jax.ShapeDtypeStruct((bsz, N_KPAIR, s, LANES), BF16),
                   jax.ShapeDtypeStruct((bsz, N_KV, s, LANES), BF16),
                   jax.ShapeDtypeStruct((bsz, Q_COLS, s), BF16),
                   jax.ShapeDtypeStruct((bsz, Q_COLS, s), BF16),
                   jax.ShapeDtypeStruct((bsz, N_VSLOT, s // KEY_CHUNK, HEAD_DIM, KEY_CHUNK), BF16),
                   jax.ShapeDtypeStruct((bsz, GATE_ROWS, s), F32)],
        compiler_params=_params(("parallel", "parallel")),
        name="mixer_proj",
    )(h, shift, scale, g.reshape(1, d), wn, wt)


def _cmp_kernel(a_ref, pa_ref, pb_ref, w1a_ref, w1b_ref, w2_ref, o_ref, ot_ref):
    a = a_ref[0, 0]
    first = jnp.dot(a, w1a_ref[...], preferred_element_type=F32)
    second = jnp.dot(a, w1b_ref[...], preferred_element_type=F32)
    bias = (jnp.dot(pa_ref[...], w1a_ref[...], preferred_element_type=F32)
            + jnp.dot(pb_ref[...], w1b_ref[...], preferred_element_type=F32))[0:1]
    n = a.shape[0]
    hid = first + pltpu.roll(second, n - 1, axis=0) + bias
    hid = jax.nn.gelu(hid)
    out = jnp.dot(hid.astype(BF16), w2_ref[...], preferred_element_type=F32)
    o_ref[0, 0] = out.astype(BF16)
    ot_ref[0, 0] = out.T.astype(BF16)


def _cmp_call(kv_chunks, pa, pb, w1a, w1b, w2):
    bsz, _, nchunk, width = kv_chunks.shape
    return pl.pallas_call(
        _cmp_kernel,
        grid=(bsz, N_KV),
        in_specs=[pl.BlockSpec((1, 1, nchunk, width), lambda b, g: (b, g, 0, 0)),
                  _const_spec(pa.shape), _const_spec(pb.shape),
                  _const_spec(w1a.shape), _const_spec(w1b.shape), _const_spec(w2.shape)],
        out_specs=[pl.BlockSpec((1, 1, nchunk, LANES), lambda b, g: (b, g, 0, 0)),
                   pl.BlockSpec((1, 1, LANES, nchunk), lambda b, g: (b, g, 0, 0))],
        out_shape=[jax.ShapeDtypeStruct((bsz, N_KV, nchunk, LANES), BF16),
                   jax.ShapeDtypeStruct((bsz, N_KV, LANES, nchunk), BF16)],
        compiler_params=_params(("parallel", "parallel")),
        name="nsa_compress",
    )(kv_chunks, pa, pb, w1a, w1b, w2)


POS_HI, POS_LO, POS_ONE = 96, 97, 98
MAX_SEL_BLOCKS = POS_HI


def _position_tables(seq):
    key = np.arange(seq)
    kpos = np.zeros((seq, LANES), np.float32)
    kpos[key, key // SEL_LEN] = 1.0
    kpos[:, POS_HI] = (key // SEL_LEN) * SEL_LEN
    kpos[:, POS_LO] = key % SEL_LEN
    kpos[:, POS_ONE] = 1.0
    ncmp = seq // CMP_STRIDE
    cpos = np.zeros((ncmp, LANES), np.float32)
    cpos[:, POS_HI] = np.arange(ncmp) * CMP_STRIDE
    cpos[:, POS_LO] = CMP_LEN - 1
    cpos[:, POS_ONE] = 1.0
    return jnp.asarray(kpos, BF16), jnp.asarray(cpos, BF16)


def _attn_kernel(sinks_ref, qn_ref, qs_ref, gn_ref, kp_ref, vt_ref, kvc_ref, kvct_ref, kpos_ref, cpos_ref,
                 ya_ref, yb_ref, s_scr, m_scr, acc_scr, *, seq, tq):
    qi = pl.program_id(1)
    t0 = qi * tq
    cols = N_REP * tq
    nsel = seq // SEL_LEN
    ncmp = kvc_ref.shape[2]
    n_top = min(SEL_TOPN, nsel)
    sub = lax.broadcasted_iota(jnp.int32, (LANES, tq), 0)
    t0f = t0.astype(F32)
    zero_q = jnp.zeros((HEAD_DIM, tq), BF16)

    def stacked_q(q_ref, g, low, sel_t=None):
        blocks = []
        for r in range(N_REP):
            h = g * N_REP + r
            q = q_ref[0, h * HEAD_DIM:(h + 1) * HEAD_DIM, :]
            qpart = jnp.concatenate([q, zero_q] if low else [zero_q, q], axis=0)
            slope = SLOPES[h]
            ext = jnp.where((sub == POS_HI) | (sub == POS_LO), slope,
                            jnp.where(sub == POS_ONE, -slope * t0f, 0.0))
            if sel_t is not None:
                ext = ext + jnp.where(sub < nsel, (sel_t - 1.0) * (-NEG_INF), 0.0)
            blocks.append(jnp.concatenate([qpart, ext.astype(BF16)], axis=0))
        return jnp.concatenate(blocks, axis=1)

    def masked(s, mask):
        return jnp.concatenate([jnp.where(mask, s[:, r * tq:(r + 1) * tq], NEG_INF) for r in range(N_REP)], axis=1)

    def value_rows(vt):
        return jnp.concatenate([jnp.ones_like(vt), vt], axis=0)

    def load_vt(slot, first_chunk, nchunks):
        return jnp.concatenate([vt_ref[0, slot, first_chunk + j] for j in range(nchunks)], axis=1)

    def band_branch(qt, pair, vslot, window, sink_row=None):
        span = tq + -(-window // KEY_CHUNK) * KEY_CHUNK
        k_start = pl.multiple_of(jnp.maximum(t0 + tq - span, 0), KEY_CHUNK)
        kaug = jnp.concatenate([kp_ref[0, pair, pl.ds(k_start, span), :], kpos_ref[pl.ds(k_start, span), :]], axis=1)
        s = jnp.dot(kaug, qt, preferred_element_type=F32)
        d = ((t0 - k_start) + lax.broadcasted_iota(jnp.int32, (span, tq), 1)
             - lax.broadcasted_iota(jnp.int32, (span, tq), 0))
        s = masked(s, (d >= 0) & (d < window))
        m = jnp.max(s, axis=0, keepdims=True)
        if sink_row is not None:
            m = jnp.maximum(m, sink_row)
        p = jnp.exp(s - m).astype(BF16)
        vt = load_vt(vslot, k_start // KEY_CHUNK, span // KEY_CHUNK)
        o = jnp.dot(value_rows(vt), p, preferred_element_type=F32)
        l = o[:HEAD_DIM]
        if sink_row is not None:
            l = l + jnp.exp(sink_row - m)
        return o[HEAD_DIM:] / l

    gsig = jax.nn.sigmoid(gn_ref[0])
    ya_rows, yb_rows = [], []

    for g in range(N_KV):
        qt_low = stacked_q(qn_ref, g, True)
        s = jnp.dot(jnp.concatenate([kvc_ref[0, g], cpos_ref[...]], axis=1), qt_low,
                    preferred_element_type=F32)
        tc = t0 + lax.broadcasted_iota(jnp.int32, (ncmp, tq), 1)
        cend = lax.broadcasted_iota(jnp.int32, (ncmp, tq), 0) * CMP_STRIDE + (CMP_LEN - 1)
        s = masked(s, tc >= cend)
        e = jnp.exp(s - jnp.max(s, axis=0, keepdims=True))
        p = e / jnp.sum(e, axis=0, keepdims=True)
        row_ok = tc >= (CMP_LEN - 1)
        p = jnp.concatenate([jnp.where(row_ok, p[:, r * tq:(r + 1) * tq], 0.0) for r in range(N_REP)], axis=1)
        o_cmp = jnp.dot(value_rows(kvct_ref[0, g, HEAD_DIM:, :]), p.astype(BF16),
                        preferred_element_type=F32)[HEAD_DIM:]
        psum = p[:, 0:tq]
        for r in range(1, N_REP):
            psum = psum + p[:, r * tq:(r + 1) * tq]

        jn = lax.broadcasted_iota(jnp.int32, (nsel, ncmp), 0) * SEL_LEN
        cn = lax.broadcasted_iota(jnp.int32, (nsel, ncmp), 1) * CMP_STRIDE
        ov_t = jnp.where((cn < jn + SEL_LEN) & (cn + CMP_LEN > jn), 1.0, 0.0).astype(BF16)
        imp_t = sum(jnp.dot(ov_t, part, preferred_element_type=F32) for part in _split3(psum))
        jb = lax.broadcasted_iota(jnp.int32, (nsel, tq), 0)
        cur = (t0 + lax.broadcasted_iota(jnp.int32, (nsel, tq), 1)) // SEL_LEN
        forced = (jb == 0) | (jb == cur) | (jb == cur - 1)
        score = jnp.where(forced, SEL_BONUS, jnp.where(jb <= cur, imp_t, -1.0))
        rank = jnp.zeros((nsel, tq), F32)
        for i in range(nsel):
            row = score[i:i + 1, :]
            gt = jnp.where(row > score, 1.0, 0.0)
            ge = jnp.where(row >= score, 1.0, 0.0)
            rank = rank + jnp.where(jb > i, ge, gt)
        sel_t = jnp.where(rank < n_top, 1.0, 0.0)
        sel_t = jnp.concatenate([sel_t, jnp.zeros((LANES - nsel, tq), F32)], axis=0)

        qt_sel = stacked_q(qn_ref, g, g == 0, sel_t)
        nck = tq // KEY_CHUNK

        def scores(kb, qt_sel=qt_sel):
            k0 = pl.multiple_of(kb * tq, tq)
            kaug = jnp.concatenate([kp_ref[0, 0, pl.ds(k0, tq), :], kpos_ref[pl.ds(k0, tq), :]], axis=1)
            return jnp.dot(kaug, qt_sel, preferred_element_type=F32)

        def fold_max(s):
            m_scr[...] = jnp.maximum(m_scr[...], jnp.max(s.reshape(tq // 8, 8, cols), axis=0))

        m_scr[...] = jnp.full((8, cols), NEG_INF, F32)

        def pass1(kb, carry):
            s = scores(kb)
            s_scr[kb] = s
            fold_max(s)
            return carry

        lax.fori_loop(0, qi, pass1, 0)
        causal = (lax.broadcasted_iota(jnp.int32, (tq, tq), 1) >= lax.broadcasted_iota(jnp.int32, (tq, tq), 0))
        s = masked(scores(qi), causal)
        s_scr[qi] = s
        fold_max(s)
        m_scr[...] = jnp.broadcast_to(jnp.max(m_scr[...], axis=0, keepdims=True), (8, cols))
        acc_scr[...] = jnp.zeros((LANES, cols), F32)

        def pass2(kb, carry, g=g):
            p = jnp.exp(s_scr[kb] - m_scr[0:1, :]).astype(BF16)
            acc_scr[...] += jnp.dot(value_rows(load_vt(g, kb * nck, nck)), p, preferred_element_type=F32)
            return carry

        lax.fori_loop(0, qi + 1, pass2, 0)
        o_slc = acc_scr[HEAD_DIM:, :] / acc_scr[:HEAD_DIM, :]

        qt_win = qt_low if g == 0 else stacked_q(qn_ref, g, False)
        o_win = band_branch(qt_win, 1, N_KV + g, NSA_WINDOW)

        for r in range(N_REP):
            c = (g * N_REP + r) * 3
            seg = slice(r * tq, (r + 1) * tq)
            ya_rows.append(gsig[c:c + 1] * o_cmp[:, seg] + gsig[c + 1:c + 2] * o_slc[:, seg]
                           + gsig[c + 2:c + 3] * o_win[:, seg])

        tlane = lax.broadcasted_iota(jnp.int32, (1, tq), 1).astype(F32)
        sink_row = jnp.concatenate(
            [sinks_ref[g * N_REP + r] + SLOPES[g * N_REP + r] * tlane for r in range(N_REP)], axis=1)
        o_swa = band_branch(stacked_q(qs_ref, g, g == 0), 2, 2 * N_KV + g, SWA_WINDOW, sink_row)
        yb_rows += [o_swa[:, r * tq:(r + 1) * tq] for r in range(N_REP)]

    ya_ref[0] = jnp.concatenate(ya_rows, axis=0).T.astype(BF16)
    yb_ref[0] = jnp.concatenate(yb_rows, axis=0).T.astype(BF16)


def _attn_call(sinks, qn_t, qs_t, gn_t, kp, vt, kvc, kvc_t, *, tq=256):
    bsz, _, s = qn_t.shape
    ncmp = kvc.shape[2]
    assert s % tq == 0 and s // SEL_LEN <= MAX_SEL_BLOCKS and s >= tq + NSA_WINDOW
    kpos, cpos = _position_tables(s)
    qspec = pl.BlockSpec((1, Q_COLS, tq), lambda b, i: (b, 0, i))
    yspec = pl.BlockSpec((1, tq, Q_COLS), lambda b, i: (b, i, 0))
    cols = N_REP * tq
    return pl.pallas_call(
        functools.partial(_attn_kernel, seq=s, tq=tq),
        grid=(bsz, s // tq),
        in_specs=[pl.BlockSpec(memory_space=pltpu.SMEM),
                  qspec, qspec,
                  pl.BlockSpec((1, GATE_ROWS, tq), lambda b, i: (b, 0, i)),
                  pl.BlockSpec((1, N_KPAIR, s, LANES), lambda b, i: (b, 0, 0, 0)),
                  pl.BlockSpec((1, N_VSLOT, s // KEY_CHUNK, HEAD_DIM, KEY_CHUNK), lambda b, i: (b, 0, 0, 0, 0)),
                  pl.BlockSpec((1, N_KV, ncmp, LANES), lambda b, i: (b, 0, 0, 0)),
                  pl.BlockSpec((1, N_KV, LANES, ncmp), lambda b, i: (b, 0, 0, 0)),
                  _const_spec((s, LANES)), _const_spec((ncmp, LANES))],
        out_specs=[yspec, yspec],
        out_shape=[jax.ShapeDtypeStruct((bsz, s, Q_COLS), BF16),
                   jax.ShapeDtypeStruct((bsz, s, Q_COLS), BF16)],
        scratch_shapes=[pltpu.VMEM((s // tq, tq, cols), F32),
                        pltpu.VMEM((8, cols), F32),
                        pltpu.VMEM((LANES, cols), F32)],
        compiler_params=_params(("parallel", "arbitrary")),
        name="hybrid_attention",
    )(sinks, qn_t, qs_t, gn_t, kp, vt, kvc, kvc_t, kpos, cpos)


def _merge_kernel(x_ref, sh_ref, sc_ref, gt_ref, g_ref, ya_ref, yb_ref, wgm_ref, wa_ref, wb_ref, wo_ref, o_ref):
    x = x_ref[0]
    d = x.shape[-1]
    u = _modulated_norm(x, g_ref[...], sh_ref[0], sc_ref[0]).astype(BF16)
    up_a = jnp.dot(ya_ref[0], wa_ref[...], preferred_element_type=F32)
    gate_a = jnp.dot(u, wgm_ref[:, :d], preferred_element_type=F32)
    merged = jax.nn.sigmoid(gate_a) * up_a
    up_b = jnp.dot(yb_ref[0], wb_ref[...], preferred_element_type=F32)
    gate_b = jnp.dot(u, wgm_ref[:, d:], preferred_element_type=F32)
    merged = merged + jax.nn.sigmoid(gate_b) * up_b
    y = jnp.dot(merged.astype(BF16), wo_ref[...], preferred_element_type=F32)
    o_ref[0] = x + gt_ref[0] * y


def _merge_call(h, shift, scale, gate, g, ya, yb, wgm, wa, wb, wo, *, tm=512):
    bsz, s, d = h.shape
    vec = pl.BlockSpec((1, 1, d), lambda b, i: (b, 0, 0))
    tok = pl.BlockSpec((1, tm, d), lambda b, i: (b, i, 0))
    ysp = pl.BlockSpec((1, tm, Q_COLS), lambda b, i: (b, i, 0))
    return pl.pallas_call(
        _merge_kernel,
        grid=(bsz, s // tm),
        in_specs=[tok, vec, vec, vec, _const_spec((1, d)), ysp, ysp,
                  _const_spec(wgm.shape), _const_spec(wa.shape), _const_spec(wb.shape), _const_spec(wo.shape)],
        out_specs=tok,
        out_shape=jax.ShapeDtypeStruct((bsz, s, d), F32),
        compiler_params=_params(("parallel", "parallel")),
        name="mixer_merge",
    )(h, shift, scale, gate, g.reshape(1, d), ya, yb, wgm, wa, wb, wo)


def _proj_column_order():
    kvw = 2 * N_KV * HEAD_DIM
    off_qn = 0
    off_c = off_qn + Q_COLS
    off_s = off_c + kvw
    off_w = off_s + kvw
    off_gn = off_w + kvw
    off_qs = off_gn + 3 * N_HEADS
    off_b = off_qs + Q_COLS
    off_gm = off_b + kvw
    half = N_KV * HEAD_DIM
    nn_cols = []
    for off in (off_s, off_w, off_b):
        nn_cols += list(range(off, off + half))
    for g in range(N_KV):
        nn_cols += list(range(off_c + g * HEAD_DIM, off_c + (g + 1) * HEAD_DIM))
        nn_cols += list(range(off_c + half + g * HEAD_DIM, off_c + half + (g + 1) * HEAD_DIM))
    t_cols = list(range(off_qn, off_qn + Q_COLS)) + list(range(off_qs, off_qs + Q_COLS))
    for off in (off_s, off_w, off_b):
        t_cols += list(range(off + half, off + 2 * half))
    t_cols += list(range(off_gn, off_gn + 3 * N_HEADS))
    return np.asarray(nn_cols, np.int32), np.asarray(t_cols, np.int32), off_gm


def _compress_weights(pos_k, w1_k, w2_k, pos_v, w1_v, w2_v):
    half = CMP_LEN // 2
    zk = jnp.zeros((half, HEAD_DIM, CMP_HIDDEN), F32)

    def w1_half(sl):
        wk = jnp.concatenate([w1_k[sl], zk], axis=-1)
        wv = jnp.concatenate([zk, w1_v[sl]], axis=-1)
        return jnp.concatenate([wk, wv], axis=1).reshape(half * 2 * HEAD_DIM, 2 * CMP_HIDDEN).astype(BF16)

    def pos_half(sl):
        p = jnp.concatenate([pos_k[sl], pos_v[sl]], axis=1).reshape(1, half * 2 * HEAD_DIM)
        return jnp.broadcast_to(p, (8, p.shape[1])).astype(BF16)

    z2 = jnp.zeros((CMP_HIDDEN, HEAD_DIM), F32)
    w2 = jnp.concatenate([jnp.concatenate([w2_k, z2], axis=1),
                          jnp.concatenate([z2, w2_v], axis=1)], axis=0).astype(BF16)
    lo, hi = slice(0, half), slice(half, CMP_LEN)
    return pos_half(lo), pos_half(hi), w1_half(lo), w1_half(hi), w2


def kernel(x, c, w_ada, b_ada, g_ffn1, w1_gate, w1_up, w1_down, g_mix, w_in, cmp_pos_k, cmp_w1_k, cmp_w2_k,
           cmp_pos_v, cmp_w1_v, cmp_w2_v, sinks, w_up_a, w_up_b, w_out, g_ffn2, w2_gate, w2_up, w2_down, g_final):
    bsz, seq, d = x.shape
    depth = w_ada.shape[0]
    nn_cols, t_cols, off_gm = _proj_column_order()
    h = x
    for l in range(depth):
        mod = _ada_call(c, w_ada[l], b_ada[l])
        sh1, sc1, gt1, sh2, sc2, gt2, sh3, sc3, gt3 = [m.reshape(bsz, 1, d) for m in jnp.split(mod, 9, axis=-1)]
        last = l == depth - 1

        h = _ffn_call(h, sh1, sc1, gt1, g_ffn1[l],
                      w1_gate[l].astype(BF16), w1_up[l].astype(BF16), w1_down[l].astype(BF16))

        w_nn = w_in[l][:, nn_cols].astype(BF16)
        w_t = jnp.concatenate([w_in[l][:, t_cols], jnp.zeros((d, T_ROWS - t_cols.shape[0]), F32)],
                              axis=1).T.astype(BF16)
        kp, kvc_in, qn_t, qs_t, vt, gn_t = _proj_call(h, sh2, sc2, g_mix[l], w_nn, w_t)

        pa, pb, w1a, w1b, w2c = _compress_weights(cmp_pos_k[l], cmp_w1_k[l], cmp_w2_k[l],
                                                  cmp_pos_v[l], cmp_w1_v[l], cmp_w2_v[l])
        nchunk = seq // CMP_STRIDE
        kvc, kvc_t = _cmp_call(kvc_in.reshape(bsz, N_KV, nchunk, CMP_STRIDE * LANES), pa, pb, w1a, w1b, w2c)

        ya, yb = _attn_call(sinks[l].reshape(-1), qn_t, qs_t, gn_t, kp, vt, kvc, kvc_t)

        h = _merge_call(h, sh2, sc2, gt2, g_mix[l], ya, yb,
                        w_in[l][:, off_gm:].astype(BF16), w_up_a[l].astype(BF16),
                        w_up_b[l].astype(BF16), w_out[l].astype(BF16))

        h = _ffn_call(h, sh3, sc3, gt3, g_ffn2[l],
                      w2_gate[l].astype(BF16), w2_up[l].astype(BF16), w2_down[l].astype(BF16),
                      g_final if last else None)
    return h
```

```python
import functools

import numpy as np
import jax
import jax.numpy as jnp
from jax import lax
from jax.experimental import pallas as pl
from jax.experimental.pallas import tpu as pltpu

F32 = jnp.float32
BF16 = jnp.bfloat16

HEAD_DIM = 64
N_HEADS = 8
N_KV = 2
N_REP = N_HEADS // N_KV
CMP_LEN = 32
CMP_STRIDE = 16
CMP_HIDDEN = 256
SEL_LEN = 64
SEL_TOPN = 8
NSA_WINDOW = 512
SWA_WINDOW = 128
FFN_RES = 0.5
RMS_EPS = 1e-6
NEG_INF = -1e30
SEL_BONUS = 1e4
ATTN_SCALE = HEAD_DIM ** -0.5

LANES = 128
VMEM_LIMIT = 56 * 1024 * 1024

SLOPES = [2.0 ** (-8.0 * (h + 1) / N_HEADS) for h in range(N_HEADS)]
LOG2E = 1.4426950408889634


def _bf16_terms(x, n=3):
    terms = []
    for _ in range(n):
        t = float(np.asarray(x, np.float32).astype(BF16).astype(np.float32))
        terms.append(t)
        x = x - t
    return terms


SLOPE_TERMS = [_bf16_terms(s * LOG2E) for s in SLOPES]


def _const_spec(shape):
    n = len(shape)
    return pl.BlockSpec(shape, lambda *_: (0,) * n, pipeline_mode=pl.Buffered(1))


def _params(sem):
    return pltpu.CompilerParams(dimension_semantics=sem, vmem_limit_bytes=VMEM_LIMIT)


def _modulated_norm(x, g, shift, scale):
    ms = jnp.mean(x * x, axis=-1, keepdims=True)
    y = x * lax.rsqrt(ms + RMS_EPS)
    return (y * g) * (1.0 + scale) + shift


def _split3(a):
    hi = a.astype(BF16)
    r1 = a - hi.astype(F32)
    mid = r1.astype(BF16)
    lo = (r1 - mid.astype(F32)).astype(BF16)
    return hi, mid, lo


def _ada_kernel(c_ref, w_ref, b_ref, o_ref):
    c = c_ref[...]
    a = c * jax.nn.sigmoid(c)
    a_hi = a.astype(BF16)
    a_lo = (a - a_hi.astype(F32)).astype(BF16)
    w = w_ref[...]
    w_hi = w.astype(BF16)
    w_lo = (w - w_hi.astype(F32)).astype(BF16)
    acc = jnp.dot(a_hi, w_hi, preferred_element_type=F32)
    acc += jnp.dot(a_hi, w_lo, preferred_element_type=F32)
    acc += jnp.dot(a_lo, w_hi, preferred_element_type=F32)
    o_ref[...] = acc + b_ref[...]


def _ada_call(c, w, b):
    bsz, d = c.shape
    n = w.shape[1]
    tn = 1024
    return pl.pallas_call(
        _ada_kernel,
        grid=(n // tn,),
        in_specs=[pl.BlockSpec((bsz, d), lambda j: (0, 0)),
                  pl.BlockSpec((d, tn), lambda j: (0, j)),
                  pl.BlockSpec((1, tn), lambda j: (0, j))],
        out_specs=pl.BlockSpec((bsz, tn), lambda j: (0, j)),
        out_shape=jax.ShapeDtypeStruct((bsz, n), F32),
        compiler_params=_params(("parallel",)),
        name="adaln",
    )(c, w, b.reshape(1, n))


def _ffn_kernel(x_ref, sh_ref, sc_ref, gt_ref, g_ref, wg_ref, wu_ref, wd_ref, *rest, tf, final):
    o_ref = rest[-1]
    x = x_ref[0]
    u = _modulated_norm(x, g_ref[...], sh_ref[0], sc_ref[0]).astype(BF16)
    dff = wg_ref.shape[1]
    acc = None
    for c in range(dff // tf):
        cols = slice(c * tf, (c + 1) * tf)
        gate = jnp.dot(u, wg_ref[:, cols], preferred_element_type=F32)
        up = jnp.dot(u, wu_ref[:, cols], preferred_element_type=F32)
        act = (gate * jax.nn.sigmoid(gate) * up).astype(BF16)
        part = jnp.dot(act, wd_ref[cols, :], preferred_element_type=F32)
        acc = part if acc is None else acc + part
    h = x + (FFN_RES * gt_ref[0]) * acc
    if final:
        gfin_ref = rest[0]
        ms = jnp.mean(h * h, axis=-1, keepdims=True)
        h = (h * lax.rsqrt(ms + RMS_EPS)) * gfin_ref[...]
    o_ref[0] = h


def _ffn_call(h, shift, scale, gate, g, wg, wu, wd, g_final=None, *, tm=512, tf=256):
    bsz, s, d = h.shape
    dff = wg.shape[1]
    final = g_final is not None
    vec = pl.BlockSpec((1, 1, d), lambda b, i: (b, 0, 0))
    in_specs = [pl.BlockSpec((1, tm, d), lambda b, i: (b, i, 0)), vec, vec, vec,
                _const_spec((1, d)), _const_spec((d, dff)), _const_spec((d, dff)), _const_spec((dff, d))]
    args = [h, shift, scale, gate, g.reshape(1, d), wg, wu, wd]
    if final:
        in_specs.append(_const_spec((1, d)))
        args.append(g_final.reshape(1, d))
    return pl.pallas_call(
        functools.partial(_ffn_kernel, tf=tf, final=final),
        grid=(bsz, s // tm),
        in_specs=in_specs,
        out_specs=pl.BlockSpec((1, tm, d), lambda b, i: (b, i, 0)),
        out_shape=jax.ShapeDtypeStruct((bsz, s, d), F32),
        compiler_params=_params(("parallel", "parallel")),
        name="ffn_final" if final else "ffn",
    )(*args)


N_KPAIR = 3
Q_COLS = N_HEADS * HEAD_DIM
NN_COLS = (N_KPAIR + N_KV) * LANES
N_VSLOT = N_KPAIR * N_KV
GATE_ROWS = 32
VT_ROWS = N_VSLOT * HEAD_DIM
T_ROWS = 2 * Q_COLS + VT_ROWS + GATE_ROWS
KEY_CHUNK = LANES


def _proj_kernel(x_ref, sh_ref, sc_ref, g_ref, wn_ref, wt_ref, kp_ref, kvc_ref, qn_ref, qs_ref, vt_ref, gn_ref):
    u = _modulated_norm(x_ref[0], g_ref[...], sh_ref[0], sc_ref[0]).astype(BF16)
    tm = u.shape[0]
    nn = jnp.dot(u, wn_ref[...], preferred_element_type=F32)
    for i in range(N_KPAIR):
        kp_ref[0, i] = nn[:, i * LANES:(i + 1) * LANES].astype(BF16)
    for i in range(N_KV):
        kvc_ref[0, i] = nn[:, (N_KPAIR + i) * LANES:(N_KPAIR + i + 1) * LANES].astype(BF16)
    tt = lax.dot_general(wt_ref[...], u, (((1,), (1,)), ((), ())), preferred_element_type=F32)
    qn_ref[0] = (tt[0:Q_COLS] * (ATTN_SCALE * LOG2E)).astype(BF16)
    qs_ref[0] = (tt[Q_COLS:2 * Q_COLS] * (ATTN_SCALE * LOG2E)).astype(BF16)
    base = 2 * Q_COLS
    for s in range(N_VSLOT):
        for c in range(tm // KEY_CHUNK):
            vt_ref[0, s, c] = tt[base + s * HEAD_DIM: base + (s + 1) * HEAD_DIM,
                                 c * KEY_CHUNK:(c + 1) * KEY_CHUNK].astype(BF16)
    gn_ref[0] = tt[base + VT_ROWS:]


def _proj_call(h, shift, scale, g, wn, wt, *, tm=512):
    bsz, s, d = h.shape
    vec = pl.BlockSpec((1, 1, d), lambda b, i: (b, 0, 0))
    nck = tm // KEY_CHUNK
    return pl.pallas_call(
        _proj_kernel,
        grid=(bsz, s // tm),
        in_specs=[pl.BlockSpec((1, tm, d), lambda b, i: (b, i, 0)), vec, vec,
                  _const_spec((1, d)), _const_spec((d, NN_COLS)), _const_spec((T_ROWS, d))],
        out_specs=[pl.BlockSpec((1, N_KPAIR, tm, LANES), lambda b, i: (b, 0, i, 0)),
                   pl.BlockSpec((1, N_KV, tm, LANES), lambda b, i: (b, 0, i, 0)),
                   pl.BlockSpec((1, Q_COLS, tm), lambda b, i: (b, 0, i)),
                   pl.BlockSpec((1, Q_COLS, tm), lambda b, i: (b, 0, i)),
                   pl.BlockSpec((1, N_VSLOT, nck, HEAD_DIM, KEY_CHUNK), lambda b, i: (b, 0, i, 0, 0)),
                   pl.BlockSpec((1, GATE_ROWS, tm), lambda b, i: (b, 0, i))],
        out_shape=[jax.ShapeDtypeStruct((bsz, N_KPAIR, s, LANES), BF16),
                   jax.ShapeDtypeStruct((bsz, N_KV, s, LANES), BF16),
                   jax.ShapeDtypeStruct((bsz, Q_COLS, s), BF16),
                   jax.ShapeDtypeStruct((bsz, Q_COLS, s), BF16),
                   jax.ShapeDtypeStruct((bsz, N_VSLOT, s // KEY_CHUNK, HEAD_DIM, KEY_CHUNK), BF16),
                   jax.ShapeDtypeStruct((bsz, GATE_ROWS, s), F32)],
        compiler_params=_params(("parallel", "parallel")),
        name="mixer_proj",
    )(h, shift, scale, g.reshape(1, d), wn, wt)


def _cmp_kernel(a_ref, pa_ref, pb_ref, w1a_ref, w1b_ref, w2_ref, o_ref, ot_ref):
    a = a_ref[0, 0]
    first = jnp.dot(a, w1a_ref[...], preferred_element_type=F32)
    second = jnp.dot(a, w1b_ref[...], preferred_element_type=F32)
    bias = (jnp.dot(pa_ref[...], w1a_ref[...], preferred_element_type=F32)
            + jnp.dot(pb_ref[...], w1b_ref[...], preferred_element_type=F32))[0:1]
    n = a.shape[0]
    hid = first + pltpu.roll(second, n - 1, axis=0) + bias
    hid = jax.nn.gelu(hid)
    out = jnp.dot(hid.astype(BF16), w2_ref[...], preferred_element_type=F32)
    o_ref[0, 0] = out.astype(BF16)
    ot_ref[0, 0] = out.T.astype(BF16)


def _cmp_call(kv_chunks, pa, pb, w1a, w1b, w2):
    bsz, _, nchunk, width = kv_chunks.shape
    return pl.pallas_call(
        _cmp_kernel,
        grid=(bsz, N_KV),
        in_specs=[pl.BlockSpec((1, 1, nchunk, width), lambda b, g: (b, g, 0, 0)),
                  _const_spec(pa.shape), _const_spec(pb.shape),
                  _const_spec(w1a.shape), _const_spec(w1b.shape), _const_spec(w2.shape)],
        out_specs=[pl.BlockSpec((1, 1, nchunk, LANES), lambda b, g: (b, g, 0, 0)),
                   pl.BlockSpec((1, 1, LANES, nchunk), lambda b, g: (b, g, 0, 0))],
        out_shape=[jax.ShapeDtypeStruct((bsz, N_KV, nchunk, LANES), BF16),
                   jax.ShapeDtypeStruct((bsz, N_KV, LANES, nchunk), BF16)],
        compiler_params=_params(("parallel", "parallel")),
        name="nsa_compress",
    )(kv_chunks, pa, pb, w1a, w1b, w2)


POS_HI, POS_LO = 96, 99
MAX_SEL_BLOCKS = POS_HI


def _position_tables(seq):
    key = np.arange(seq)
    kpos = np.zeros((seq, LANES), np.float32)
    kpos[key, key // SEL_LEN] = 1.0
    kpos[:, POS_HI:POS_HI + 3] = ((key // SEL_LEN) * SEL_LEN)[:, None]
    kpos[:, POS_LO:POS_LO + 3] = (key % SEL_LEN)[:, None]
    ncmp = seq // CMP_STRIDE
    cpos = np.zeros((ncmp, LANES), np.float32)
    cpos[:, POS_HI:POS_HI + 3] = (np.arange(ncmp) * CMP_STRIDE)[:, None]
    cpos[:, POS_LO:POS_LO + 3] = CMP_LEN - 1
    return jnp.asarray(kpos, BF16), jnp.asarray(cpos, BF16)


def _attn_kernel(sinks_ref, qn_ref, qs_ref, gn_ref, kp_ref, vt_ref, kvc_ref, kvct_ref, kpos_ref, cpos_ref,
                 ya_ref, yb_ref, s_scr, m_scr, acc_scr, *, seq, tq, tqs):
    qi = pl.program_id(1)
    t0 = qi * tq
    cols = N_REP * tq
    nsel = seq // SEL_LEN
    ncmp = kvc_ref.shape[2]
    n_top = min(SEL_TOPN, nsel)

    def stacked_q(q_ref, g, low, width, lane0=0, sel_t=None):
        sub = lax.broadcasted_iota(jnp.int32, (LANES, width), 0)
        zero_q = jnp.zeros((HEAD_DIM, width), BF16)
        blocks = []
        for r in range(N_REP):
            h = g * N_REP + r
            q = q_ref[0, h * HEAD_DIM:(h + 1) * HEAD_DIM, lane0:lane0 + width]
            qpart = jnp.concatenate([q, zero_q] if low else [zero_q, q], axis=0)
            ext = jnp.zeros((LANES, width), F32)
            for i, c in enumerate(SLOPE_TERMS[h]):
                ext = jnp.where((sub == POS_HI + i) | (sub == POS_LO + i), c, ext)
            if sel_t is not None:
                ext = ext + jnp.where(sub < nsel, (sel_t - 1.0) * (-NEG_INF), 0.0)
            blocks.append(jnp.concatenate([qpart, ext.astype(BF16)], axis=0))
        return jnp.concatenate(blocks, axis=1)

    def masked(s, mask, width):
        return jnp.concatenate(
            [jnp.where(mask, s[:, r * width:(r + 1) * width], NEG_INF) for r in range(N_REP)], axis=1)

    def value_rows(vt):
        return jnp.concatenate([jnp.ones_like(vt), vt], axis=0)

    def load_vt(slot, first_chunk, nchunks):
        return jnp.concatenate([vt_ref[0, slot, first_chunk + j] for j in range(nchunks)], axis=1)

    def band_geometry(window, sub):
        span = tqs + -(-window // KEY_CHUNK) * KEY_CHUNK
        ts = t0 + sub * tqs
        k_start = pl.multiple_of(jnp.maximum(ts + tqs - span, 0), KEY_CHUNK)
        d = ((ts - k_start) + lax.broadcasted_iota(jnp.int32, (span, tqs), 1)
             - lax.broadcasted_iota(jnp.int32, (span, tqs), 0))
        return span, k_start, (d >= 0) & (d < window)

    def band_branch(qt, pair, vslot, geometry, sink_row=None):
        span, k_start, mask = geometry
        kaug = jnp.concatenate([kp_ref[0, pair, pl.ds(k_start, span), :], kpos_ref[pl.ds(k_start, span), :]], axis=1)
        s = masked(jnp.dot(kaug, qt, preferred_element_type=F32), mask, tqs)
        m = jnp.max(s, axis=0, keepdims=True)
        if sink_row is not None:
            m = jnp.maximum(m, sink_row)
        p = jnp.exp2(s - m).astype(BF16)
        vt = load_vt(vslot, k_start // KEY_CHUNK, span // KEY_CHUNK)
        o = jnp.dot(value_rows(vt), p, preferred_element_type=F32)
        l = o[:HEAD_DIM]
        if sink_row is not None:
            l = l + jnp.exp2(sink_row - m)
        return o[HEAD_DIM:] / l

    tc = t0 + lax.broadcasted_iota(jnp.int32, (ncmp, tq), 1)
    cend = lax.broadcasted_iota(jnp.int32, (ncmp, tq), 0) * CMP_STRIDE + (CMP_LEN - 1)
    cmask = tc >= cend
    row_ok = tc >= (CMP_LEN - 1)
    jn = lax.broadcasted_iota(jnp.int32, (nsel, ncmp), 0) * SEL_LEN
    cn = lax.broadcasted_iota(jnp.int32, (nsel, ncmp), 1) * CMP_STRIDE
    ov_t = jnp.where((cn < jn + SEL_LEN) & (cn + CMP_LEN > jn), 1.0, 0.0).astype(BF16)
    jb = lax.broadcasted_iota(jnp.int32, (nsel, tq), 0)
    cur = (t0 + lax.broadcasted_iota(jnp.int32, (nsel, tq), 1)) // SEL_LEN
    forced = (jb == 0) | (jb == cur) | (jb == cur - 1)
    valid = jb <= cur
    o_cmp, qt_sel = [], []
    for g in range(N_KV):
        s = jnp.dot(jnp.concatenate([kvc_ref[0, g], cpos_ref[...]], axis=1), stacked_q(qn_ref, g, True, tq),
                    preferred_element_type=F32)
        s = masked(s, cmask, tq)
        e = jnp.exp2(s - jnp.max(s, axis=0, keepdims=True))
        p = e / jnp.sum(e, axis=0, keepdims=True)
        p = jnp.concatenate([jnp.where(row_ok, p[:, r * tq:(r + 1) * tq], 0.0) for r in range(N_REP)], axis=1)
        o_cmp.append(jnp.dot(value_rows(kvct_ref[0, g, HEAD_DIM:, :]), p.astype(BF16),
                             preferred_element_type=F32)[HEAD_DIM:])
        psum = p[:, 0:tq]
        for r in range(1, N_REP):
            psum = psum + p[:, r * tq:(r + 1) * tq]
        imp_t = sum(jnp.dot(ov_t, part, preferred_element_type=F32) for part in _split3(psum))
        score = jnp.where(forced, SEL_BONUS, jnp.where(valid, imp_t, -1.0))
        rank = [jnp.zeros((8, tq), F32) for _ in range(nsel // 8)]
        for i in range(nsel):
            row = score[i:i + 1, :]
            for k in range(nsel // 8):
                blk = score[8 * k:8 * (k + 1)]
                ge = jnp.where(row >= blk, 1.0, 0.0)
                gt = jnp.where(row > blk, 1.0, 0.0)
                if 8 * k > i:
                    beats = ge
                elif 8 * k + 7 <= i:
                    beats = gt
                else:
                    beats = jnp.where(lax.broadcasted_iota(jnp.int32, (8, tq), 0) > i - 8 * k, ge, gt)
                rank[k] = rank[k] + beats
        sel_t = jnp.where(jnp.concatenate(rank, axis=0) < n_top, 1.0, 0.0)
        sel_t = jnp.concatenate([sel_t, jnp.zeros((LANES - nsel, tq), F32)], axis=0)
        qt_sel.append(stacked_q(qn_ref, g, g == 0, tq, 0, sel_t))

    nck = tq // KEY_CHUNK

    def scores(kb, g):
        k0 = pl.multiple_of(kb * tq, tq)
        kaug = jnp.concatenate([kp_ref[0, 0, pl.ds(k0, tq), :], kpos_ref[pl.ds(k0, tq), :]], axis=1)
        return jnp.dot(kaug, qt_sel[g], preferred_element_type=F32)

    def fold_max(s, g):
        m_scr[g] = jnp.maximum(m_scr[g], jnp.max(s.reshape(tq // 8, 8, cols), axis=0))

    for g in range(N_KV):
        m_scr[g] = jnp.full((8, cols), NEG_INF, F32)
        acc_scr[g] = jnp.zeros((LANES, cols), F32)

    def pass1(kb, carry):
        for g in range(N_KV):
            s = scores(kb, g)
            s_scr[g, kb] = s
            fold_max(s, g)
        return carry

    lax.fori_loop(0, qi, pass1, 0)
    causal = (lax.broadcasted_iota(jnp.int32, (tq, tq), 1) >= lax.broadcasted_iota(jnp.int32, (tq, tq), 0))
    for g in range(N_KV):
        s = masked(scores(qi, g), causal, tq)
        s_scr[g, qi] = s
        fold_max(s, g)
        m_scr[g] = jnp.broadcast_to(jnp.max(m_scr[g], axis=0, keepdims=True), (8, cols))

    def pass2(kb, carry):
        for g in range(N_KV):
            p = jnp.exp2(s_scr[g, kb] - m_scr[g, 0:1, :]).astype(BF16)
            acc_scr[g] += jnp.dot(value_rows(load_vt(g, kb * nck, nck)), p, preferred_element_type=F32)
        return carry

    lax.fori_loop(0, qi + 1, pass2, 0)

    gsig = jax.nn.sigmoid(gn_ref[0])
    ya_cols, yb_cols = [], []
    for sub in range(tq // tqs):
        lane0 = sub * tqs
        geo_win = band_geometry(NSA_WINDOW, sub)
        geo_swa = band_geometry(SWA_WINDOW, sub)
        tpos = (t0 + lane0 + lax.broadcasted_iota(jnp.int32, (1, tqs), 1)).astype(F32)
        ya_rows, yb_rows = [], []
        for g in range(N_KV):
            o_win = band_branch(stacked_q(qn_ref, g, g == 0, tqs, lane0), 1, N_KV + g, geo_win)
            o_slc = acc_scr[g, HEAD_DIM:, :] / acc_scr[g, :HEAD_DIM, :]
            for r in range(N_REP):
                c = (g * N_REP + r) * 3
                wide = slice(r * tq + lane0, r * tq + lane0 + tqs)
                gates = gsig[:, lane0:lane0 + tqs]
                ya_rows.append(gates[c:c + 1] * o_cmp[g][:, wide] + gates[c + 1:c + 2] * o_slc[:, wide]
                               + gates[c + 2:c + 3] * o_win[:, r * tqs:(r + 1) * tqs])
            sink_row = jnp.concatenate(
                [sinks_ref[g * N_REP + r] * LOG2E + (SLOPES[g * N_REP + r] * LOG2E) * tpos for r in range(N_REP)],
                axis=1)
            o_swa = band_branch(stacked_q(qs_ref, g, g == 0, tqs, lane0), 2, 2 * N_KV + g, geo_swa, sink_row)
            yb_rows += [o_swa[:, r * tqs:(r + 1) * tqs] for r in range(N_REP)]
        ya_cols.append(jnp.concatenate(ya_rows, axis=0))
        yb_cols.append(jnp.concatenate(yb_rows, axis=0))
    ya_ref[0] = jnp.concatenate(ya_cols, axis=1).T.astype(BF16)
    yb_ref[0] = jnp.concatenate(yb_cols, axis=1).T.astype(BF16)


def _attn_call(sinks, qn_t, qs_t, gn_t, kp, vt, kvc, kvc_t, *, tq=256, tqs=128):
    bsz, _, s = qn_t.shape
    ncmp = kvc.shape[2]
    assert s % tq == 0 and s // SEL_LEN <= MAX_SEL_BLOCKS and s >= tq + NSA_WINDOW
    kpos, cpos = _position_tables(s)
    qspec = pl.BlockSpec((1, Q_COLS, tq), lambda b, i: (b, 0, i))
    yspec = pl.BlockSpec((1, tq, Q_COLS), lambda b, i: (b, i, 0))
    cols = N_REP * tq
    return pl.pallas_call(
        functools.partial(_attn_kernel, seq=s, tq=tq, tqs=tqs),
        grid=(bsz, s // tq),
        in_specs=[pl.BlockSpec(memory_space=pltpu.SMEM),
                  qspec, qspec,
                  pl.BlockSpec((1, GATE_ROWS, tq), lambda b, i: (b, 0, i)),
                  pl.BlockSpec((1, N_KPAIR, s, LANES), lambda b, i: (b, 0, 0, 0)),
                  pl.BlockSpec((1, N_VSLOT, s // KEY_CHUNK, HEAD_DIM, KEY_CHUNK), lambda b, i: (b, 0, 0, 0, 0)),
                  pl.BlockSpec((1, N_KV, ncmp, LANES), lambda b, i: (b, 0, 0, 0)),
                  pl.BlockSpec((1, N_KV, LANES, ncmp), lambda b, i: (b, 0, 0, 0)),
                  _const_spec((s, LANES)), _const_spec((ncmp, LANES))],
        out_specs=[yspec, yspec],
        out_shape=[jax.ShapeDtypeStruct((bsz, s, Q_COLS), BF16),
                   jax.ShapeDtypeStruct((bsz, s, Q_COLS), BF16)],
        scratch_shapes=[pltpu.VMEM((N_KV, s // tq, tq, cols), F32),
                        pltpu.VMEM((N_KV, 8, cols), F32),
                        pltpu.VMEM((N_KV, LANES, cols), F32)],
        compiler_params=_params(("parallel", "arbitrary")),
        name="hybrid_attention",
    )(sinks, qn_t, qs_t, gn_t, kp, vt, kvc, kvc_t, kpos, cpos)


def _merge_kernel(x_ref, sh_ref, sc_ref, gt_ref, g_ref, ya_ref, yb_ref, wgm_ref, wa_ref, wb_ref, wo_ref, o_ref):
    x = x_ref[0]
    d = x.shape[-1]
    u = _modulated_norm(x, g_ref[...], sh_ref[0], sc_ref[0]).astype(BF16)
    up_a = jnp.dot(ya_ref[0], wa_ref[...], preferred_element_type=F32)
    gate_a = jnp.dot(u, wgm_ref[:, :d], preferred_element_type=F32)
    merged = jax.nn.sigmoid(gate_a) * up_a
    up_b = jnp.dot(yb_ref[0], wb_ref[...], preferred_element_type=F32)
    gate_b = jnp.dot(u, wgm_ref[:, d:], preferred_element_type=F32)
    merged = merged + jax.nn.sigmoid(gate_b) * up_b
    y = jnp.dot(merged.astype(BF16), wo_ref[...], preferred_element_type=F32)
    o_ref[0] = x + gt_ref[0] * y


def _merge_call(h, shift, scale, gate, g, ya, yb, wgm, wa, wb, wo, *, tm=512):
    bsz, s, d = h.shape
    vec = pl.BlockSpec((1, 1, d), lambda b, i: (b, 0, 0))
    tok = pl.BlockSpec((1, tm, d), lambda b, i: (b, i, 0))
    ysp = pl.BlockSpec((1, tm, Q_COLS), lambda b, i: (b, i, 0))
    return pl.pallas_call(
        _merge_kernel,
        grid=(bsz, s // tm),
        in_specs=[tok, vec, vec, vec, _const_spec((1, d)), ysp, ysp,
                  _const_spec(wgm.shape), _const_spec(wa.shape), _const_spec(wb.shape), _const_spec(wo.shape)],
        out_specs=tok,
        out_shape=jax.ShapeDtypeStruct((bsz, s, d), F32),
        compiler_params=_params(("parallel", "parallel")),
        name="mixer_merge",
    )(h, shift, scale, gate, g.reshape(1, d), ya, yb, wgm, wa, wb, wo)


def _proj_column_order():
    kvw = 2 * N_KV * HEAD_DIM
    off_qn = 0
    off_c = off_qn + Q_COLS
    off_s = off_c + kvw
    off_w = off_s + kvw
    off_gn = off_w + kvw
    off_qs = off_gn + 3 * N_HEADS
    off_b = off_qs + Q_COLS
    off_gm = off_b + kvw
    half = N_KV * HEAD_DIM
    nn_cols = []
    for off in (off_s, off_w, off_b):
        nn_cols += list(range(off, off + half))
    for g in range(N_KV):
        nn_cols += list(range(off_c + g * HEAD_DIM, off_c + (g + 1) * HEAD_DIM))
        nn_cols += list(range(off_c + half + g * HEAD_DIM, off_c + half + (g + 1) * HEAD_DIM))
    t_cols = list(range(off_qn, off_qn + Q_COLS)) + list(range(off_qs, off_qs + Q_COLS))
    for off in (off_s, off_w, off_b):
        t_cols += list(range(off + half, off + 2 * half))
    t_cols += list(range(off_gn, off_gn + 3 * N_HEADS))
    return np.asarray(nn_cols, np.int32), np.asarray(t_cols, np.int32), off_gm


def _compress_weights(pos_k, w1_k, w2_k, pos_v, w1_v, w2_v):
    half = CMP_LEN // 2
    zk = jnp.zeros((half, HEAD_DIM, CMP_HIDDEN), F32)

    def w1_half(sl):
        wk = jnp.concatenate([w1_k[sl], zk], axis=-1)
        wv = jnp.concatenate([zk, w1_v[sl]], axis=-1)
        return jnp.concatenate([wk, wv], axis=1).reshape(half * 2 * HEAD_DIM, 2 * CMP_HIDDEN).astype(BF16)

    def pos_half(sl):
        p = jnp.concatenate([pos_k[sl], pos_v[sl]], axis=1).reshape(1, half * 2 * HEAD_DIM)
        return jnp.broadcast_to(p, (8, p.shape[1])).astype(BF16)

    z2 = jnp.zeros((CMP_HIDDEN, HEAD_DIM), F32)
    w2 = jnp.concatenate([jnp.concatenate([w2_k, z2], axis=1),
                          jnp.concatenate([z2, w2_v], axis=1)], axis=0).astype(BF16)
    lo, hi = slice(0, half), slice(half, CMP_LEN)
    return pos_half(lo), pos_half(hi), w1_half(lo), w1_half(hi), w2


def kernel(x, c, w_ada, b_ada, g_ffn1, w1_gate, w1_up, w1_down, g_mix, w_in, cmp_pos_k, cmp_w1_k, cmp_w2_k,
           cmp_pos_v, cmp_w1_v, cmp_w2_v, sinks, w_up_a, w_up_b, w_out, g_ffn2, w2_gate, w2_up, w2_down, g_final):
    bsz, seq, d = x.shape
    depth = w_ada.shape[0]
    nn_cols, t_cols, off_gm = _proj_column_order()
    h = x
    for l in range(depth):
        mod = _ada_call(c, w_ada[l], b_ada[l])
        sh1, sc1, gt1, sh2, sc2, gt2, sh3, sc3, gt3 = [m.reshape(bsz, 1, d) for m in jnp.split(mod, 9, axis=-1)]
        last = l == depth - 1

        h = _ffn_call(h, sh1, sc1, gt1, g_ffn1[l],
                      w1_gate[l].astype(BF16), w1_up[l].astype(BF16), w1_down[l].astype(BF16))

        w_nn = w_in[l][:, nn_cols].astype(BF16)
        w_t = jnp.concatenate([w_in[l][:, t_cols], jnp.zeros((d, T_ROWS - t_cols.shape[0]), F32)],
                              axis=1).T.astype(BF16)
        kp, kvc_in, qn_t, qs_t, vt, gn_t = _proj_call(h, sh2, sc2, g_mix[l], w_nn, w_t)

        pa, pb, w1a, w1b, w2c = _compress_weights(cmp_pos_k[l], cmp_w1_k[l], cmp_w2_k[l],
                                                  cmp_pos_v[l], cmp_w1_v[l], cmp_w2_v[l])
        nchunk = seq // CMP_STRIDE
        kvc, kvc_t = _cmp_call(kvc_in.reshape(bsz, N_KV, nchunk, CMP_STRIDE * LANES), pa, pb, w1a, w1b, w2c)

        ya, yb = _attn_call(sinks[l].reshape(-1), qn_t, qs_t, gn_t, kp, vt, kvc, kvc_t)

        h = _merge_call(h, sh2, sc2, gt2, g_mix[l], ya, yb,
                        w_in[l][:, off_gm:].astype(BF16), w_up_a[l].astype(BF16),
                        w_up_b[l].astype(BF16), w_out[l].astype(BF16))

        h = _ffn_call(h, sh3, sc3, gt3, g_ffn2[l],
                      w2_gate[l].astype(BF16), w2_up[l].astype(BF16), w2_down[l].astype(BF16),
                      g_final if last else None)
    return h
```

```python
import functools

import numpy as np
import jax
import jax.numpy as jnp
from jax import lax
from jax.experimental import pallas as pl
from jax.experimental.pallas import tpu as pltpu

F32 = jnp.float32
BF16 = jnp.bfloat16

HEAD_DIM = 64
N_HEADS = 8
N_KV = 2
N_REP = N_HEADS // N_KV
CMP_LEN = 32
CMP_STRIDE = 16
CMP_HIDDEN = 256
SEL_LEN = 64
SEL_TOPN = 8
NSA_WINDOW = 512
SWA_WINDOW = 128
FFN_RES = 0.5
RMS_EPS = 1e-6
NEG_INF = -1e30
SEL_BONUS = 1e4
ATTN_SCALE = HEAD_DIM ** -0.5

LANES = 128
VMEM_LIMIT = 56 * 1024 * 1024

SLOPES = [2.0 ** (-8.0 * (h + 1) / N_HEADS) for h in range(N_HEADS)]
LOG2E = 1.4426950408889634


def _bf16_terms(x, n=3):
    terms = []
    for _ in range(n):
        t = float(np.asarray(x, np.float32).astype(BF16).astype(np.float32))
        terms.append(t)
        x = x - t
    return terms


SLOPE_TERMS = [_bf16_terms(s * LOG2E) for s in SLOPES]


def _const_spec(shape):
    n = len(shape)
    return pl.BlockSpec(shape, lambda *_: (0,) * n, pipeline_mode=pl.Buffered(1))


def _params(sem):
    return pltpu.CompilerParams(dimension_semantics=sem, vmem_limit_bytes=VMEM_LIMIT)


def _modulated_norm(x, g, shift, scale):
    ms = jnp.mean(x * x, axis=-1, keepdims=True)
    y = x * lax.rsqrt(ms + RMS_EPS)
    return (y * g) * (1.0 + scale) + shift


def _split3(a):
    hi = a.astype(BF16)
    r1 = a - hi.astype(F32)
    mid = r1.astype(BF16)
    lo = (r1 - mid.astype(F32)).astype(BF16)
    return hi, mid, lo


def _ada_kernel(c_ref, w_ref, b_ref, o_ref):
    c = c_ref[...]
    a = c * jax.nn.sigmoid(c)
    a_hi = a.astype(BF16)
    a_lo = (a - a_hi.astype(F32)).astype(BF16)
    w = w_ref[...]
    w_hi = w.astype(BF16)
    w_lo = (w - w_hi.astype(F32)).astype(BF16)
    acc = jnp.dot(a_hi, w_hi, preferred_element_type=F32)
    acc += jnp.dot(a_hi, w_lo, preferred_element_type=F32)
    acc += jnp.dot(a_lo, w_hi, preferred_element_type=F32)
    o_ref[...] = acc + b_ref[...]


def _ada_call(c, w, b):
    bsz, d = c.shape
    n = w.shape[1]
    tn = 1024
    return pl.pallas_call(
        _ada_kernel,
        grid=(n // tn,),
        in_specs=[pl.BlockSpec((bsz, d), lambda j: (0, 0)),
                  pl.BlockSpec((d, tn), lambda j: (0, j)),
                  pl.BlockSpec((1, tn), lambda j: (0, j))],
        out_specs=pl.BlockSpec((bsz, tn), lambda j: (0, j)),
        out_shape=jax.ShapeDtypeStruct((bsz, n), F32),
        compiler_params=_params(("parallel",)),
        name="adaln",
    )(c, w, b.reshape(1, n))


def _ffn_kernel(x_ref, sh_ref, sc_ref, gt_ref, g_ref, wg_ref, wu_ref, wd_ref, *rest, tf, final):
    o_ref = rest[-1]
    x = x_ref[0]
    u = _modulated_norm(x, g_ref[...], sh_ref[0], sc_ref[0]).astype(BF16)
    dff = wg_ref.shape[1]
    acc = None
    for c in range(dff // tf):
        cols = slice(c * tf, (c + 1) * tf)
        gate = jnp.dot(u, wg_ref[:, cols], preferred_element_type=F32)
        up = jnp.dot(u, wu_ref[:, cols], preferred_element_type=F32)
        act = (gate * jax.nn.sigmoid(gate) * up).astype(BF16)
        part = jnp.dot(act, wd_ref[cols, :], preferred_element_type=F32)
        acc = part if acc is None else acc + part
    h = x + (FFN_RES * gt_ref[0]) * acc
    if final:
        gfin_ref = rest[0]
        ms = jnp.mean(h * h, axis=-1, keepdims=True)
        h = (h * lax.rsqrt(ms + RMS_EPS)) * gfin_ref[...]
    o_ref[0] = h


def _ffn_call(h, shift, scale, gate, g, wg, wu, wd, g_final=None, *, tm=512, tf=256):
    bsz, s, d = h.shape
    dff = wg.shape[1]
    final = g_final is not None
    vec = pl.BlockSpec((1, 1, d), lambda b, i: (b, 0, 0))
    in_specs = [pl.BlockSpec((1, tm, d), lambda b, i: (b, i, 0)), vec, vec, vec,
                _const_spec((1, d)), _const_spec((d, dff)), _const_spec((d, dff)), _const_spec((dff, d))]
    args = [h, shift, scale, gate, g.reshape(1, d), wg, wu, wd]
    if final:
        in_specs.append(_const_spec((1, d)))
        args.append(g_final.reshape(1, d))
    return pl.pallas_call(
        functools.partial(_ffn_kernel, tf=tf, final=final),
        grid=(bsz, s // tm),
        in_specs=in_specs,
        out_specs=pl.BlockSpec((1, tm, d), lambda b, i: (b, i, 0)),
        out_shape=jax.ShapeDtypeStruct((bsz, s, d), F32),
        compiler_params=_params(("parallel", "parallel")),
        name="ffn_final" if final else "ffn",
    )(*args)


N_KPAIR = 3
Q_COLS = N_HEADS * HEAD_DIM
NN_COLS = (N_KPAIR + N_KV) * LANES
N_VSLOT = N_KPAIR * N_KV
GATE_ROWS = 32
VT_ROWS = N_VSLOT * HEAD_DIM
T_ROWS = 2 * Q_COLS + VT_ROWS + GATE_ROWS
KEY_CHUNK = LANES


def _proj_kernel(x_ref, sh_ref, sc_ref, g_ref, wn_ref, wt_ref, kp_ref, kvc_ref, qn_ref, qs_ref, vt_ref, gn_ref,
                 slab_scr):
    u = _modulated_norm(x_ref[0], g_ref[...], sh_ref[0], sc_ref[0]).astype(BF16)
    tm = u.shape[0]
    nn = jnp.dot(u, wn_ref[...], preferred_element_type=F32)
    for i in range(N_KPAIR):
        kp_ref[0, i] = nn[:, i * LANES:(i + 1) * LANES].astype(BF16)
    for i in range(N_KV):
        slab_scr[...] = nn[:, (N_KPAIR + i) * LANES:(N_KPAIR + i + 1) * LANES]
        for t in range(CMP_STRIDE):
            rows = slab_scr[pl.ds(t, tm // CMP_STRIDE, stride=CMP_STRIDE), :]
            kvc_ref[0, i, :, t * LANES:(t + 1) * LANES] = rows.astype(BF16)
    tt = lax.dot_general(wt_ref[...], u, (((1,), (1,)), ((), ())), preferred_element_type=F32)
    qn_ref[0] = (tt[0:Q_COLS] * (ATTN_SCALE * LOG2E)).astype(BF16)
    qs_ref[0] = (tt[Q_COLS:2 * Q_COLS] * (ATTN_SCALE * LOG2E)).astype(BF16)
    base = 2 * Q_COLS
    for s in range(N_VSLOT):
        for c in range(tm // KEY_CHUNK):
            vt_ref[0, s, c] = tt[base + s * HEAD_DIM: base + (s + 1) * HEAD_DIM,
                                 c * KEY_CHUNK:(c + 1) * KEY_CHUNK].astype(BF16)
    gn_ref[0] = tt[base + VT_ROWS:]


def _proj_call(h, shift, scale, g, wn, wt, *, tm=512):
    bsz, s, d = h.shape
    vec = pl.BlockSpec((1, 1, d), lambda b, i: (b, 0, 0))
    nck = tm // KEY_CHUNK
    return pl.pallas_call(
        _proj_kernel,
        grid=(bsz, s // tm),
        in_specs=[pl.BlockSpec((1, tm, d), lambda b, i: (b, i, 0)), vec, vec,
                  _const_spec((1, d)), _const_spec((d, NN_COLS)), _const_spec((T_ROWS, d))],
        out_specs=[pl.BlockSpec((1, N_KPAIR, tm, LANES), lambda b, i: (b, 0, i, 0)),
                   pl.BlockSpec((1, N_KV, tm // CMP_STRIDE, CMP_STRIDE * LANES), lambda b, i: (b, 0, i, 0)),
                   pl.BlockSpec((1, Q_COLS, tm), lambda b, i: (b, 0, i)),
                   pl.BlockSpec((1, Q_COLS, tm), lambda b, i: (b, 0, i)),
                   pl.BlockSpec((1, N_VSLOT, nck, HEAD_DIM, KEY_CHUNK), lambda b, i: (b, 0, i, 0, 0)),
                   pl.BlockSpec((1, GATE_ROWS, tm), lambda b, i: (b, 0, i))],
        out_shape=[jax.ShapeDtypeStruct((bsz, N_KPAIR, s, LANES), BF16),
                   jax.ShapeDtypeStruct((bsz, N_KV, s // CMP_STRIDE, CMP_STRIDE * LANES), BF16),
                   jax.ShapeDtypeStruct((bsz, Q_COLS, s), BF16),
                   jax.ShapeDtypeStruct((bsz, Q_COLS, s), BF16),
                   jax.ShapeDtypeStruct((bsz, N_VSLOT, s // KEY_CHUNK, HEAD_DIM, KEY_CHUNK), BF16),
                   jax.ShapeDtypeStruct((bsz, GATE_ROWS, s), F32)],
        scratch_shapes=[pltpu.VMEM((tm, LANES), F32)],
        compiler_params=_params(("parallel", "parallel")),
        name="mixer_proj",
    )(h, shift, scale, g.reshape(1, d), wn, wt)


def _cmp_kernel(a_ref, pa_ref, pb_ref, w1a_ref, w1b_ref, w2_ref, o_ref, ot_ref):
    a = a_ref[0, 0]
    first = jnp.dot(a, w1a_ref[...], preferred_element_type=F32)
    second = jnp.dot(a, w1b_ref[...], preferred_element_type=F32)
    bias = (jnp.dot(pa_ref[...], w1a_ref[...], preferred_element_type=F32)
            + jnp.dot(pb_ref[...], w1b_ref[...], preferred_element_type=F32))[0:1]
    n = a.shape[0]
    hid = first + pltpu.roll(second, n - 1, axis=0) + bias
    hid = jax.nn.gelu(hid)
    out = jnp.dot(hid.astype(BF16), w2_ref[...], preferred_element_type=F32)
    o_ref[0, 0] = out.astype(BF16)
    ot_ref[0, 0] = out.T.astype(BF16)


def _cmp_call(kv_chunks, pa, pb, w1a, w1b, w2):
    bsz, _, nchunk, width = kv_chunks.shape
    return pl.pallas_call(
        _cmp_kernel,
        grid=(bsz, N_KV),
        in_specs=[pl.BlockSpec((1, 1, nchunk, width), lambda b, g: (b, g, 0, 0)),
                  _const_spec(pa.shape), _const_spec(pb.shape),
                  _const_spec(w1a.shape), _const_spec(w1b.shape), _const_spec(w2.shape)],
        out_specs=[pl.BlockSpec((1, 1, nchunk, LANES), lambda b, g: (b, g, 0, 0)),
                   pl.BlockSpec((1, 1, LANES, nchunk), lambda b, g: (b, g, 0, 0))],
        out_shape=[jax.ShapeDtypeStruct((bsz, N_KV, nchunk, LANES), BF16),
                   jax.ShapeDtypeStruct((bsz, N_KV, LANES, nchunk), BF16)],
        compiler_params=_params(("parallel", "parallel")),
        name="nsa_compress",
    )(kv_chunks, pa, pb, w1a, w1b, w2)


POS_HI, POS_LO, POS_ONE = 96, 99, 102
MAX_SEL_BLOCKS = POS_HI


def _position_tables(seq):
    key = np.arange(seq)
    kpos = np.zeros((seq, LANES), np.float32)
    kpos[key, key // SEL_LEN] = 1.0
    kpos[:, POS_HI:POS_HI + 3] = ((key // SEL_LEN) * SEL_LEN)[:, None]
    kpos[:, POS_LO:POS_LO + 3] = (key % SEL_LEN)[:, None]
    kpos[:, POS_ONE:POS_ONE + 3] = 1.0
    ncmp = seq // CMP_STRIDE
    cpos = np.zeros((ncmp, LANES), np.float32)
    cpos[:, POS_HI:POS_HI + 3] = (np.arange(ncmp) * CMP_STRIDE)[:, None]
    cpos[:, POS_LO:POS_LO + 3] = CMP_LEN - 1
    cpos[:, POS_ONE:POS_ONE + 3] = 1.0
    return jnp.asarray(kpos, BF16), jnp.asarray(cpos, BF16)


def _attn_kernel(sinks_ref, qn_ref, qs_ref, gn_ref, kp_ref, vt_ref, kvc_ref, kvct_ref, kpos_ref, cpos_ref,
                 ya_ref, yb_ref, s_scr, m_scr, acc_scr, *, seq, tq, tqs):
    qi = pl.program_id(1)
    t0 = qi * tq
    cols = N_REP * tq
    nsel = seq // SEL_LEN
    ncmp = kvc_ref.shape[2]
    n_top = min(SEL_TOPN, nsel)

    def stacked_q(q_ref, g, low, width, lane0=0, sel_t=None):
        sub = lax.broadcasted_iota(jnp.int32, (LANES, width), 0)
        zero_q = jnp.zeros((HEAD_DIM, width), BF16)
        blocks = []
        for r in range(N_REP):
            h = g * N_REP + r
            q = q_ref[0, h * HEAD_DIM:(h + 1) * HEAD_DIM, lane0:lane0 + width]
            qpart = jnp.concatenate([q, zero_q] if low else [zero_q, q], axis=0)
            ext = jnp.zeros((LANES, width), F32)
            for i, c in enumerate(SLOPE_TERMS[h]):
                ext = jnp.where((sub == POS_HI + i) | (sub == POS_LO + i), c, ext)
            tpos = (t0 + lane0 + lax.broadcasted_iota(jnp.int32, (1, width), 1)).astype(F32)
            for i, part in enumerate(_split3(-(SLOPES[h] * LOG2E) * tpos)):
                ext = jnp.where(sub == POS_ONE + i, part.astype(F32), ext)
            if sel_t is not None:
                ext = ext + jnp.where(sub < nsel, (sel_t - 1.0) * (-NEG_INF), 0.0)
            blocks.append(jnp.concatenate([qpart, ext.astype(BF16)], axis=0))
        return jnp.concatenate(blocks, axis=1)

    def masked(s, mask, width):
        return jnp.concatenate(
            [jnp.where(mask, s[:, r * width:(r + 1) * width], NEG_INF) for r in range(N_REP)], axis=1)

    def biased(s, bias, width):
        sb = s.astype(BF16)
        return jnp.concatenate([sb[:, r * width:(r + 1) * width] + bias for r in range(N_REP)], axis=1)

    def value_rows(vt):
        return jnp.concatenate([jnp.ones_like(vt), vt], axis=0)

    def load_vt(slot, first_chunk, nchunks):
        return jnp.concatenate([vt_ref[0, slot, first_chunk + j] for j in range(nchunks)], axis=1)

    def band_geometry(window, sub):
        span = tqs + -(-window // KEY_CHUNK) * KEY_CHUNK
        ts = t0 + sub * tqs
        k_start = pl.multiple_of(jnp.maximum(ts + tqs - span, 0), KEY_CHUNK)
        d = ((ts - k_start) + lax.broadcasted_iota(jnp.int32, (span, tqs), 1)
             - lax.broadcasted_iota(jnp.int32, (span, tqs), 0))
        return span, k_start, jnp.where((d >= 0) & (d < window), 0.0, NEG_INF).astype(BF16)

    def band_scores(qt, pair, geometry):
        span, k_start, bias = geometry
        kaug = jnp.concatenate([kp_ref[0, pair, pl.ds(k_start, span), :], kpos_ref[pl.ds(k_start, span), :]], axis=1)
        return biased(jnp.dot(kaug, qt, preferred_element_type=F32), bias, tqs)

    def band_output(s, vslot, geometry, sink_row=None):
        span, k_start, _ = geometry
        m = jnp.max(s, axis=0, keepdims=True)
        if sink_row is not None:
            m = jnp.maximum(m, sink_row)
        p = jnp.exp2(s - m)
        vt = load_vt(vslot, k_start // KEY_CHUNK, span // KEY_CHUNK)
        o = jnp.dot(value_rows(vt), p, preferred_element_type=F32)
        l = o[:HEAD_DIM]
        if sink_row is not None:
            l = l + jnp.exp2(sink_row - m).astype(F32)
        return o[HEAD_DIM:] / l

    tc = t0 + lax.broadcasted_iota(jnp.int32, (ncmp, tq), 1)
    cend = lax.broadcasted_iota(jnp.int32, (ncmp, tq), 0) * CMP_STRIDE + (CMP_LEN - 1)
    cmask = tc >= cend
    row_ok = tc >= (CMP_LEN - 1)
    jn = lax.broadcasted_iota(jnp.int32, (nsel, ncmp), 0) * SEL_LEN
    cn = lax.broadcasted_iota(jnp.int32, (nsel, ncmp), 1) * CMP_STRIDE
    ov_t = jnp.where((cn < jn + SEL_LEN) & (cn + CMP_LEN > jn), 1.0, 0.0).astype(BF16)
    jb = lax.broadcasted_iota(jnp.int32, (nsel, tq), 0)
    cur = (t0 + lax.broadcasted_iota(jnp.int32, (nsel, tq), 1)) // SEL_LEN
    forced = (jb == 0) | (jb == cur) | (jb == cur - 1)
    valid = jb <= cur
    cmp_scores = [jnp.dot(jnp.concatenate([kvc_ref[0, g], cpos_ref[...]], axis=1), stacked_q(qn_ref, g, True, tq),
                          preferred_element_type=F32) for g in range(N_KV)]
    o_cmp, scores_sel = [], []
    for g in range(N_KV):
        s = masked(cmp_scores[g], cmask, tq)
        e = jnp.exp2(s - jnp.max(s, axis=0, keepdims=True))
        p = e / jnp.sum(e, axis=0, keepdims=True)
        p = jnp.concatenate([jnp.where(row_ok, p[:, r * tq:(r + 1) * tq], 0.0) for r in range(N_REP)], axis=1)
        o_cmp.append(jnp.dot(value_rows(kvct_ref[0, g, HEAD_DIM:, :]), p.astype(BF16),
                             preferred_element_type=F32)[HEAD_DIM:])
        psum = p[:, 0:tq]
        for r in range(1, N_REP):
            psum = psum + p[:, r * tq:(r + 1) * tq]
        imp_t = sum(jnp.dot(ov_t, part, preferred_element_type=F32) for part in _split3(psum))
        scores_sel.append(jnp.where(forced, SEL_BONUS, jnp.where(valid, imp_t, -1.0)))

    items = []
    for sub in range(tq // tqs):
        geo_win = band_geometry(NSA_WINDOW, sub)
        geo_swa = band_geometry(SWA_WINDOW, sub)
        for g in range(N_KV):
            sink_row = jnp.concatenate(
                [jnp.full((1, tqs), sinks_ref[g * N_REP + r] * LOG2E, F32) for r in range(N_REP)],
                axis=1).astype(BF16)
            items.append((qn_ref, g, sub, 1, N_KV + g, geo_win, None))
            items.append((qs_ref, g, sub, 2, 2 * N_KV + g, geo_swa, sink_row))

    def item_scores(item):
        q_ref, g, sub, pair, _, geo, _ = item
        return band_scores(stacked_q(q_ref, g, g == 0, tqs, sub * tqs), pair, geo)

    outs = []
    s_next = item_scores(items[0])
    for k, item in enumerate(items):
        s_cur = s_next
        if k + 1 < len(items):
            s_next = item_scores(items[k + 1])
        outs.append(band_output(s_cur, item[4], item[5], item[6]))

    qt_sel = []
    for score in scores_sel:
        rank = [jnp.zeros((8, tq), F32) for _ in range(nsel // 8)]
        for i in range(nsel):
            row = score[i:i + 1, :]
            for k in range(nsel // 8):
                blk = score[8 * k:8 * (k + 1)]
                ge = jnp.where(row >= blk, 1.0, 0.0)
                gt = jnp.where(row > blk, 1.0, 0.0)
                if 8 * k > i:
                    beats = ge
                elif 8 * k + 7 <= i:
                    beats = gt
                else:
                    beats = jnp.where(lax.broadcasted_iota(jnp.int32, (8, tq), 0) > i - 8 * k, ge, gt)
                rank[k] = rank[k] + beats
        sel_t = jnp.where(jnp.concatenate(rank, axis=0) < n_top, 1.0, 0.0)
        sel_t = jnp.concatenate([sel_t, jnp.zeros((LANES - nsel, tq), F32)], axis=0)
        qt_sel.append(stacked_q(qn_ref, len(qt_sel), len(qt_sel) == 0, tq, 0, sel_t))

    nck = tq // KEY_CHUNK

    def score_step(kb, n, bias=None):
        k0 = pl.multiple_of(kb * tq, tq)
        kaug = jnp.concatenate([kp_ref[0, 0, pl.ds(k0, n * tq), :], kpos_ref[pl.ds(k0, n * tq), :]], axis=1)
        raw = [jnp.dot(kaug, qt_sel[g], preferred_element_type=F32) for g in range(N_KV)]
        for g in range(N_KV):
            s = raw[g].astype(BF16) if bias is None else biased(raw[g], bias, tq)
            s_scr[g, pl.ds(kb, n)] = s.reshape(n, tq, cols)
            m_scr[g] = jnp.maximum(m_scr[g], jnp.max(s.reshape(n * tq // 16, 16, cols), axis=0))

    def value_step(kb, n):
        ps = [jnp.exp2(s_scr[g, pl.ds(kb, n)].reshape(n * tq, cols) - m_scr[g, 0:1, :]) for g in range(N_KV)]
        for g in range(N_KV):
            acc_scr[g] += jnp.dot(value_rows(load_vt(g, kb * nck, n * nck)), ps[g], preferred_element_type=F32)

    for g in range(N_KV):
        m_scr[g] = jnp.full((16, cols), NEG_INF, BF16)
        acc_scr[g] = jnp.zeros((LANES, cols), F32)

    def pass1(i, carry):
        score_step(2 * i, 2)
        return carry

    lax.fori_loop(0, qi // 2, pass1, 0)
    causal = jnp.where(lax.broadcasted_iota(jnp.int32, (tq, tq), 1) >= lax.broadcasted_iota(jnp.int32, (tq, tq), 0),
                       0.0, NEG_INF).astype(BF16)
    odd = (qi % 2) == 1

    @pl.when(odd)
    def _():
        score_step(qi - 1, 2, jnp.concatenate([jnp.zeros((tq, tq), BF16), causal], axis=0))

    @pl.when(jnp.logical_not(odd))
    def _():
        score_step(qi, 1, causal)

    for g in range(N_KV):
        m_scr[g] = jnp.broadcast_to(jnp.max(m_scr[g], axis=0, keepdims=True), (16, cols))

    def pass2(i, carry):
        value_step(2 * i, 2)
        return carry

    lax.fori_loop(0, (qi + 1) // 2, pass2, 0)

    @pl.when(jnp.logical_not(odd))
    def _():
        value_step(qi, 1)

    gsig = jax.nn.sigmoid(gn_ref[0])
    ya_cols, yb_cols = [], []
    for sub in range(tq // tqs):
        lane0 = sub * tqs
        gates = gsig[:, lane0:lane0 + tqs]
        ya_rows, yb_rows = [], []
        for g in range(N_KV):
            o_win, o_swa = outs[(sub * N_KV + g) * 2], outs[(sub * N_KV + g) * 2 + 1]
            o_slc = acc_scr[g, HEAD_DIM:, :] / acc_scr[g, :HEAD_DIM, :]
            for r in range(N_REP):
                c = (g * N_REP + r) * 3
                wide = slice(r * tq + lane0, r * tq + lane0 + tqs)
                ya_rows.append(gates[c:c + 1] * o_cmp[g][:, wide] + gates[c + 1:c + 2] * o_slc[:, wide]
                               + gates[c + 2:c + 3] * o_win[:, r * tqs:(r + 1) * tqs])
            yb_rows += [o_swa[:, r * tqs:(r + 1) * tqs] for r in range(N_REP)]
        ya_cols.append(jnp.concatenate(ya_rows, axis=0))
        yb_cols.append(jnp.concatenate(yb_rows, axis=0))
    ya_ref[0] = jnp.concatenate(ya_cols, axis=1).T.astype(BF16)
    yb_ref[0] = jnp.concatenate(yb_cols, axis=1).T.astype(BF16)


def _attn_call(sinks, qn_t, qs_t, gn_t, kp, vt, kvc, kvc_t, *, tq=256, tqs=128):
    bsz, _, s = qn_t.shape
    ncmp = kvc.shape[2]
    assert s % tq == 0 and s // SEL_LEN <= MAX_SEL_BLOCKS and s >= tq + NSA_WINDOW
    kpos, cpos = _position_tables(s)
    qspec = pl.BlockSpec((1, Q_COLS, tq), lambda b, i: (b, 0, i))
    yspec = pl.BlockSpec((1, tq, Q_COLS), lambda b, i: (b, i, 0))
    cols = N_REP * tq
    return pl.pallas_call(
        functools.partial(_attn_kernel, seq=s, tq=tq, tqs=tqs),
        grid=(bsz, s // tq),
        in_specs=[pl.BlockSpec(memory_space=pltpu.SMEM),
                  qspec, qspec,
                  pl.BlockSpec((1, GATE_ROWS, tq), lambda b, i: (b, 0, i)),
                  pl.BlockSpec((1, N_KPAIR, s, LANES), lambda b, i: (b, 0, 0, 0)),
                  pl.BlockSpec((1, N_VSLOT, s // KEY_CHUNK, HEAD_DIM, KEY_CHUNK), lambda b, i: (b, 0, 0, 0, 0)),
                  pl.BlockSpec((1, N_KV, ncmp, LANES), lambda b, i: (b, 0, 0, 0)),
                  pl.BlockSpec((1, N_KV, LANES, ncmp), lambda b, i: (b, 0, 0, 0)),
                  _const_spec((s, LANES)), _const_spec((ncmp, LANES))],
        out_specs=[yspec, yspec],
        out_shape=[jax.ShapeDtypeStruct((bsz, s, Q_COLS), BF16),
                   jax.ShapeDtypeStruct((bsz, s, Q_COLS), BF16)],
        scratch_shapes=[pltpu.VMEM((N_KV, s // tq, tq, cols), BF16),
                        pltpu.VMEM((N_KV, 16, cols), BF16),
                        pltpu.VMEM((N_KV, LANES, cols), F32)],
        compiler_params=_params(("parallel", "arbitrary")),
        name="hybrid_attention",
    )(sinks, qn_t, qs_t, gn_t, kp, vt, kvc, kvc_t, kpos, cpos)


def _merge_kernel(x_ref, sh_ref, sc_ref, gt_ref, g_ref, ya_ref, yb_ref, wgm_ref, wa_ref, wb_ref, wo_ref, o_ref):
    x = x_ref[0]
    d = x.shape[-1]
    u = _modulated_norm(x, g_ref[...], sh_ref[0], sc_ref[0]).astype(BF16)
    up_a = jnp.dot(ya_ref[0], wa_ref[...], preferred_element_type=F32)
    gate_a = jnp.dot(u, wgm_ref[:, :d], preferred_element_type=F32)
    merged = jax.nn.sigmoid(gate_a) * up_a
    up_b = jnp.dot(yb_ref[0], wb_ref[...], preferred_element_type=F32)
    gate_b = jnp.dot(u, wgm_ref[:, d:], preferred_element_type=F32)
    merged = merged + jax.nn.sigmoid(gate_b) * up_b
    y = jnp.dot(merged.astype(BF16), wo_ref[...], preferred_element_type=F32)
    o_ref[0] = x + gt_ref[0] * y


def _merge_call(h, shift, scale, gate, g, ya, yb, wgm, wa, wb, wo, *, tm=512):
    bsz, s, d = h.shape
    vec = pl.BlockSpec((1, 1, d), lambda b, i: (b, 0, 0))
    tok = pl.BlockSpec((1, tm, d), lambda b, i: (b, i, 0))
    ysp = pl.BlockSpec((1, tm, Q_COLS), lambda b, i: (b, i, 0))
    return pl.pallas_call(
        _merge_kernel,
        grid=(bsz, s // tm),
        in_specs=[tok, vec, vec, vec, _const_spec((1, d)), ysp, ysp,
                  _const_spec(wgm.shape), _const_spec(wa.shape), _const_spec(wb.shape), _const_spec(wo.shape)],
        out_specs=tok,
        out_shape=jax.ShapeDtypeStruct((bsz, s, d), F32),
        compiler_params=_params(("parallel", "parallel")),
        name="mixer_merge",
    )(h, shift, scale, gate, g.reshape(1, d), ya, yb, wgm, wa, wb, wo)


def _proj_column_ranges():
    kvw = 2 * N_KV * HEAD_DIM
    off_qn = 0
    off_c = off_qn + Q_COLS
    off_s = off_c + kvw
    off_w = off_s + kvw
    off_gn = off_w + kvw
    off_qs = off_gn + 3 * N_HEADS
    off_b = off_qs + Q_COLS
    off_gm = off_b + kvw
    half = N_KV * HEAD_DIM
    nn = [(off, off + half) for off in (off_s, off_w, off_b)]
    for g in range(N_KV):
        nn += [(off_c + g * HEAD_DIM, off_c + (g + 1) * HEAD_DIM),
               (off_c + half + g * HEAD_DIM, off_c + half + (g + 1) * HEAD_DIM)]
    tt = [(off_qn, off_qn + Q_COLS), (off_qs, off_qs + Q_COLS)]
    tt += [(off + half, off + 2 * half) for off in (off_s, off_w, off_b)]
    tt += [(off_gn, off_gn + 3 * N_HEADS)]
    return nn, tt, off_gm


def _take_columns(w, ranges, width):
    parts = [w[:, a:b] for a, b in ranges]
    have = sum(b - a for a, b in ranges)
    if width > have:
        parts.append(jnp.zeros((w.shape[0], width - have), w.dtype))
    return jnp.concatenate(parts, axis=1)


def _compress_weights(pos_k, w1_k, w2_k, pos_v, w1_v, w2_v):
    half = CMP_LEN // 2
    zk = jnp.zeros((half, HEAD_DIM, CMP_HIDDEN), F32)

    def w1_half(sl):
        wk = jnp.concatenate([w1_k[sl], zk], axis=-1)
        wv = jnp.concatenate([zk, w1_v[sl]], axis=-1)
        return jnp.concatenate([wk, wv], axis=1).reshape(half * 2 * HEAD_DIM, 2 * CMP_HIDDEN).astype(BF16)

    def pos_half(sl):
        p = jnp.concatenate([pos_k[sl], pos_v[sl]], axis=1).reshape(1, half * 2 * HEAD_DIM)
        return jnp.broadcast_to(p, (8, p.shape[1])).astype(BF16)

    z2 = jnp.zeros((CMP_HIDDEN, HEAD_DIM), F32)
    w2 = jnp.concatenate([jnp.concatenate([w2_k, z2], axis=1),
                          jnp.concatenate([z2, w2_v], axis=1)], axis=0).astype(BF16)
    lo, hi = slice(0, half), slice(half, CMP_LEN)
    return pos_half(lo), pos_half(hi), w1_half(lo), w1_half(hi), w2


def kernel(x, c, w_ada, b_ada, g_ffn1, w1_gate, w1_up, w1_down, g_mix, w_in, cmp_pos_k, cmp_w1_k, cmp_w2_k,
           cmp_pos_v, cmp_w1_v, cmp_w2_v, sinks, w_up_a, w_up_b, w_out, g_ffn2, w2_gate, w2_up, w2_down, g_final):
    bsz, seq, d = x.shape
    depth = w_ada.shape[0]
    nn_ranges, t_ranges, off_gm = _proj_column_ranges()
    h = x
    for l in range(depth):
        mod = _ada_call(c, w_ada[l], b_ada[l])
        sh1, sc1, gt1, sh2, sc2, gt2, sh3, sc3, gt3 = [m.reshape(bsz, 1, d) for m in jnp.split(mod, 9, axis=-1)]
        last = l == depth - 1

        h = _ffn_call(h, sh1, sc1, gt1, g_ffn1[l],
                      w1_gate[l].astype(BF16), w1_up[l].astype(BF16), w1_down[l].astype(BF16))

        w_nn = _take_columns(w_in[l], nn_ranges, NN_COLS).astype(BF16)
        w_t = _take_columns(w_in[l], t_ranges, T_ROWS).T.astype(BF16)
        kp, kvc_in, qn_t, qs_t, vt, gn_t = _proj_call(h, sh2, sc2, g_mix[l], w_nn, w_t)

        pa, pb, w1a, w1b, w2c = _compress_weights(cmp_pos_k[l], cmp_w1_k[l], cmp_w2_k[l],
                                                  cmp_pos_v[l], cmp_w1_v[l], cmp_w2_v[l])
        kvc, kvc_t = _cmp_call(kvc_in, pa, pb, w1a, w1b, w2c)

        ya, yb = _attn_call(sinks[l].reshape(-1), qn_t, qs_t, gn_t, kp, vt, kvc, kvc_t)

        h = _merge_call(h, sh2, sc2, gt2, g_mix[l], ya, yb,
                        w_in[l][:, off_gm:].astype(BF16), w_up_a[l].astype(BF16),
                        w_up_b[l].astype(BF16), w_out[l].astype(BF16))

        h = _ffn_call(h, sh3, sc3, gt3, g_ffn2[l],
                      w2_gate[l].astype(BF16), w2_up[l].astype(BF16), w2_down[l].astype(BF16),
                      g_final if last else None)
    return h
```

```python
import functools

import numpy as np
import jax
import jax.numpy as jnp
from jax import lax
from jax.experimental import pallas as pl
from jax.experimental.pallas import tpu as pltpu

F32 = jnp.float32
BF16 = jnp.bfloat16

HEAD_DIM = 64
N_HEADS = 8
N_KV = 2
N_REP = N_HEADS // N_KV
CMP_LEN = 32
CMP_STRIDE = 16
CMP_HIDDEN = 256
SEL_LEN = 64
SEL_TOPN = 8
NSA_WINDOW = 512
SWA_WINDOW = 128
FFN_RES = 0.5
RMS_EPS = 1e-6
NEG_INF = -1e30
SEL_BONUS = 1e4
ATTN_SCALE = HEAD_DIM ** -0.5

LANES = 128
VMEM_LIMIT = 56 * 1024 * 1024

SLOPES = [2.0 ** (-8.0 * (h + 1) / N_HEADS) for h in range(N_HEADS)]
LOG2E = 1.4426950408889634


def _bf16_terms(x, n=3):
    terms = []
    for _ in range(n):
        t = float(np.asarray(x, np.float32).astype(BF16).astype(np.float32))
        terms.append(t)
        x = x - t
    return terms


SLOPE_TERMS = [_bf16_terms(s * LOG2E) for s in SLOPES]


def _const_spec(shape):
    n = len(shape)
    return pl.BlockSpec(shape, lambda *_: (0,) * n, pipeline_mode=pl.Buffered(1))


def _params(sem):
    return pltpu.CompilerParams(dimension_semantics=sem, vmem_limit_bytes=VMEM_LIMIT)


def _modulated_norm(x, g, shift, scale):
    ms = jnp.mean(x * x, axis=-1, keepdims=True)
    y = x * lax.rsqrt(ms + RMS_EPS)
    return (y * g) * (1.0 + scale) + shift


def _split3(a):
    hi = a.astype(BF16)
    r1 = a - hi.astype(F32)
    mid = r1.astype(BF16)
    lo = (r1 - mid.astype(F32)).astype(BF16)
    return hi, mid, lo


def _ada_kernel(c_ref, w_ref, b_ref, o_ref):
    c = c_ref[...]
    a = c * jax.nn.sigmoid(c)
    a_hi = a.astype(BF16)
    a_lo = (a - a_hi.astype(F32)).astype(BF16)
    w = w_ref[...]
    w_hi = w.astype(BF16)
    w_lo = (w - w_hi.astype(F32)).astype(BF16)
    acc = jnp.dot(a_hi, w_hi, preferred_element_type=F32)
    acc += jnp.dot(a_hi, w_lo, preferred_element_type=F32)
    acc += jnp.dot(a_lo, w_hi, preferred_element_type=F32)
    o_ref[...] = acc + b_ref[...]


def _ada_call(c, w, b):
    bsz, d = c.shape
    n = w.shape[1]
    tn = 1024
    return pl.pallas_call(
        _ada_kernel,
        grid=(n // tn,),
        in_specs=[pl.BlockSpec((bsz, d), lambda j: (0, 0)),
                  pl.BlockSpec((d, tn), lambda j: (0, j)),
                  pl.BlockSpec((1, tn), lambda j: (0, j))],
        out_specs=pl.BlockSpec((bsz, tn), lambda j: (0, j)),
        out_shape=jax.ShapeDtypeStruct((bsz, n), F32),
        compiler_params=_params(("parallel",)),
        name="adaln",
    )(c, w, b.reshape(1, n))


def _ffn_kernel(x_ref, sh_ref, sc_ref, gt_ref, g_ref, wg_ref, wu_ref, wd_ref, *rest, tf, final, nsplit):
    o_ref = rest[-1]
    rows = x_ref.shape[1] // nsplit
    dff = wg_ref.shape[1]
    for h in range(nsplit):
        seg = slice(h * rows, (h + 1) * rows)
        x = x_ref[0, seg]
        u = _modulated_norm(x, g_ref[...], sh_ref[0], sc_ref[0]).astype(BF16)
        acc = None
        for c in range(dff // tf):
            cols = slice(c * tf, (c + 1) * tf)
            gate = jnp.dot(u, wg_ref[:, cols], preferred_element_type=F32)
            up = jnp.dot(u, wu_ref[:, cols], preferred_element_type=F32)
            act = (gate * jax.nn.sigmoid(gate) * up).astype(BF16)
            part = jnp.dot(act, wd_ref[cols, :], preferred_element_type=F32)
            acc = part if acc is None else acc + part
        y = x + (FFN_RES * gt_ref[0]) * acc
        if final:
            gfin_ref = rest[0]
            ms = jnp.mean(y * y, axis=-1, keepdims=True)
            y = (y * lax.rsqrt(ms + RMS_EPS)) * gfin_ref[...]
        o_ref[0, seg] = y


def _ffn_call(h, shift, scale, gate, g, wg, wu, wd, g_final=None, *, tm=1024, tf=256, nsplit=2):
    bsz, s, d = h.shape
    dff = wg.shape[1]
    final = g_final is not None
    vec = pl.BlockSpec((1, 1, d), lambda b, i: (b, 0, 0))
    in_specs = [pl.BlockSpec((1, tm, d), lambda b, i: (b, i, 0)), vec, vec, vec,
                _const_spec((1, d)), _const_spec((d, dff)), _const_spec((d, dff)), _const_spec((dff, d))]
    args = [h, shift, scale, gate, g.reshape(1, d), wg, wu, wd]
    if final:
        in_specs.append(_const_spec((1, d)))
        args.append(g_final.reshape(1, d))
    return pl.pallas_call(
        functools.partial(_ffn_kernel, tf=tf, final=final, nsplit=nsplit),
        grid=(bsz, s // tm),
        in_specs=in_specs,
        out_specs=pl.BlockSpec((1, tm, d), lambda b, i: (b, i, 0)),
        out_shape=jax.ShapeDtypeStruct((bsz, s, d), F32),
        compiler_params=_params(("parallel", "parallel")),
        name="ffn_final" if final else "ffn",
    )(*args)


N_KPAIR = 3
Q_COLS = N_HEADS * HEAD_DIM
NN_COLS = (N_KPAIR + N_KV) * LANES
N_VSLOT = N_KPAIR * N_KV
GATE_ROWS = 32
VT_ROWS = N_VSLOT * HEAD_DIM
T_ROWS = 2 * Q_COLS + VT_ROWS + GATE_ROWS
KEY_CHUNK = LANES


def _proj_kernel(x_ref, sh_ref, sc_ref, g_ref, wn_ref, wt_ref, kp_ref, kvc_ref, qn_ref, qs_ref, vt_ref, gn_ref,
                 slab_scr, *, nsplit):
    rows = x_ref.shape[1] // nsplit

    def products(h):
        u = _modulated_norm(x_ref[0, h * rows:(h + 1) * rows], g_ref[...], sh_ref[0], sc_ref[0]).astype(BF16)
        return (jnp.dot(u, wn_ref[...], preferred_element_type=F32),
                lax.dot_general(wt_ref[...], u, (((1,), (1,)), ((), ())), preferred_element_type=F32))

    nxt = products(0)
    for h in range(nsplit):
        nn, tt = nxt
        if h + 1 < nsplit:
            nxt = products(h + 1)
        seg = slice(h * rows, (h + 1) * rows)
        for i in range(N_KPAIR):
            kp_ref[0, i, seg] = nn[:, i * LANES:(i + 1) * LANES].astype(BF16)
        crow = slice(h * rows // CMP_STRIDE, (h + 1) * rows // CMP_STRIDE)
        for i in range(N_KV):
            slab_scr[...] = nn[:, (N_KPAIR + i) * LANES:(N_KPAIR + i + 1) * LANES]
            for t in range(CMP_STRIDE):
                part = slab_scr[pl.ds(t, rows // CMP_STRIDE, stride=CMP_STRIDE), :]
                kvc_ref[0, i, crow, t * LANES:(t + 1) * LANES] = part.astype(BF16)
        qn_ref[0, :, seg] = (tt[0:Q_COLS] * (ATTN_SCALE * LOG2E)).astype(BF16)
        qs_ref[0, :, seg] = (tt[Q_COLS:2 * Q_COLS] * (ATTN_SCALE * LOG2E)).astype(BF16)
        base = 2 * Q_COLS
        for s in range(N_VSLOT):
            for c in range(rows // KEY_CHUNK):
                vt_ref[0, s, h * (rows // KEY_CHUNK) + c] = tt[base + s * HEAD_DIM: base + (s + 1) * HEAD_DIM,
                                                               c * KEY_CHUNK:(c + 1) * KEY_CHUNK].astype(BF16)
        gn_ref[0, :, seg] = tt[base + VT_ROWS:]


def _proj_call(h, shift, scale, g, wn, wt, *, tm=1024, nsplit=2):
    bsz, s, d = h.shape
    vec = pl.BlockSpec((1, 1, d), lambda b, i: (b, 0, 0))
    nck = tm // KEY_CHUNK
    return pl.pallas_call(
        functools.partial(_proj_kernel, nsplit=nsplit),
        grid=(bsz, s // tm),
        in_specs=[pl.BlockSpec((1, tm, d), lambda b, i: (b, i, 0)), vec, vec,
                  _const_spec((1, d)), _const_spec((d, NN_COLS)), _const_spec((T_ROWS, d))],
        out_specs=[pl.BlockSpec((1, N_KPAIR, tm, LANES), lambda b, i: (b, 0, i, 0)),
                   pl.BlockSpec((1, N_KV, tm // CMP_STRIDE, CMP_STRIDE * LANES), lambda b, i: (b, 0, i, 0)),
                   pl.BlockSpec((1, Q_COLS, tm), lambda b, i: (b, 0, i)),
                   pl.BlockSpec((1, Q_COLS, tm), lambda b, i: (b, 0, i)),
                   pl.BlockSpec((1, N_VSLOT, nck, HEAD_DIM, KEY_CHUNK), lambda b, i: (b, 0, i, 0, 0)),
                   pl.BlockSpec((1, GATE_ROWS, tm), lambda b, i: (b, 0, i))],
        out_shape=[jax.ShapeDtypeStruct((bsz, N_KPAIR, s, LANES), BF16),
                   jax.ShapeDtypeStruct((bsz, N_KV, s // CMP_STRIDE, CMP_STRIDE * LANES), BF16),
                   jax.ShapeDtypeStruct((bsz, Q_COLS, s), BF16),
                   jax.ShapeDtypeStruct((bsz, Q_COLS, s), BF16),
                   jax.ShapeDtypeStruct((bsz, N_VSLOT, s // KEY_CHUNK, HEAD_DIM, KEY_CHUNK), BF16),
                   jax.ShapeDtypeStruct((bsz, GATE_ROWS, s), F32)],
        scratch_shapes=[pltpu.VMEM((tm // nsplit, LANES), F32)],
        compiler_params=_params(("parallel", "parallel")),
        name="mixer_proj",
    )(h, shift, scale, g.reshape(1, d), wn, wt)


def _cmp_kernel(a_ref, pa_ref, pb_ref, w1a_ref, w1b_ref, w2_ref, o_ref, ot_ref):
    a = a_ref[0, 0]
    first = jnp.dot(a, w1a_ref[...], preferred_element_type=F32)
    second = jnp.dot(a, w1b_ref[...], preferred_element_type=F32)
    bias = (jnp.dot(pa_ref[...], w1a_ref[...], preferred_element_type=F32)
            + jnp.dot(pb_ref[...], w1b_ref[...], preferred_element_type=F32))[0:1]
    n = a.shape[0]
    hid = first + pltpu.roll(second, n - 1, axis=0) + bias
    hid = jax.nn.gelu(hid)
    out = jnp.dot(hid.astype(BF16), w2_ref[...], preferred_element_type=F32)
    o_ref[0, 0] = out.astype(BF16)
    ot_ref[0, 0] = out.T.astype(BF16)


def _cmp_call(kv_chunks, pa, pb, w1a, w1b, w2):
    bsz, _, nchunk, width = kv_chunks.shape
    return pl.pallas_call(
        _cmp_kernel,
        grid=(bsz, N_KV),
        in_specs=[pl.BlockSpec((1, 1, nchunk, width), lambda b, g: (b, g, 0, 0)),
                  _const_spec(pa.shape), _const_spec(pb.shape),
                  _const_spec(w1a.shape), _const_spec(w1b.shape), _const_spec(w2.shape)],
        out_specs=[pl.BlockSpec((1, 1, nchunk, LANES), lambda b, g: (b, g, 0, 0)),
                   pl.BlockSpec((1, 1, LANES, nchunk), lambda b, g: (b, g, 0, 0))],
        out_shape=[jax.ShapeDtypeStruct((bsz, N_KV, nchunk, LANES), BF16),
                   jax.ShapeDtypeStruct((bsz, N_KV, LANES, nchunk), BF16)],
        compiler_params=_params(("parallel", "parallel")),
        name="nsa_compress",
    )(kv_chunks, pa, pb, w1a, w1b, w2)


POS_HI, POS_LO, POS_ONE = 96, 99, 102
MAX_SEL_BLOCKS = POS_HI


def _position_tables(seq):
    key = np.arange(seq)
    kpos = np.zeros((seq, LANES), np.float32)
    kpos[key, key // SEL_LEN] = 1.0
    kpos[:, POS_HI:POS_HI + 3] = ((key // SEL_LEN) * SEL_LEN)[:, None]
    kpos[:, POS_LO:POS_LO + 3] = (key % SEL_LEN)[:, None]
    kpos[:, POS_ONE:POS_ONE + 3] = 1.0
    ncmp = seq // CMP_STRIDE
    cpos = np.zeros((ncmp, LANES), np.float32)
    cpos[:, POS_HI:POS_HI + 3] = (np.arange(ncmp) * CMP_STRIDE)[:, None]
    cpos[:, POS_LO:POS_LO + 3] = CMP_LEN - 1
    cpos[:, POS_ONE:POS_ONE + 3] = 1.0
    return jnp.asarray(kpos, BF16), jnp.asarray(cpos, BF16)


def _attn_kernel(sinks_ref, qn_ref, qs_ref, gn_ref, kp_ref, vt_ref, kvc_ref, kvct_ref, kpos_ref, cpos_ref,
                 ya_ref, yb_ref, s_scr, m_scr, acc_scr, *, seq, tq, tqs):
    qi = pl.program_id(1)
    t0 = qi * tq
    cols = N_REP * tq
    nsel = seq // SEL_LEN
    ncmp = kvc_ref.shape[2]
    n_top = min(SEL_TOPN, nsel)

    def stacked_q(q_ref, g, low, width, lane0=0, sel_t=None):
        sub = lax.broadcasted_iota(jnp.int32, (LANES, width), 0)
        zero_q = jnp.zeros((HEAD_DIM, width), BF16)
        blocks = []
        for r in range(N_REP):
            h = g * N_REP + r
            q = q_ref[0, h * HEAD_DIM:(h + 1) * HEAD_DIM, lane0:lane0 + width]
            qpart = jnp.concatenate([q, zero_q] if low else [zero_q, q], axis=0)
            ext = jnp.zeros((LANES, width), F32)
            for i, c in enumerate(SLOPE_TERMS[h]):
                ext = jnp.where((sub == POS_HI + i) | (sub == POS_LO + i), c, ext)
            tpos = (t0 + lane0 + lax.broadcasted_iota(jnp.int32, (1, width), 1)).astype(F32)
            for i, part in enumerate(_split3(-(SLOPES[h] * LOG2E) * tpos)):
                ext = jnp.where(sub == POS_ONE + i, part.astype(F32), ext)
            if sel_t is not None:
                ext = ext + jnp.where(sub < nsel, (sel_t - 1.0) * (-NEG_INF), 0.0)
            blocks.append(jnp.concatenate([qpart, ext.astype(BF16)], axis=0))
        return jnp.concatenate(blocks, axis=1)

    def masked(s, mask, width):
        return jnp.concatenate(
            [jnp.where(mask, s[:, r * width:(r + 1) * width], NEG_INF) for r in range(N_REP)], axis=1)

    def biased(s, bias, width):
        sb = s.astype(BF16)
        return jnp.concatenate([sb[:, r * width:(r + 1) * width] + bias for r in range(N_REP)], axis=1)

    def value_rows(vt):
        return jnp.concatenate([jnp.ones_like(vt), vt], axis=0)

    def load_vt(slot, first_chunk, nchunks):
        return jnp.concatenate([vt_ref[0, slot, first_chunk + j] for j in range(nchunks)], axis=1)

    def band_geometry(window, sub):
        span = tqs + -(-window // KEY_CHUNK) * KEY_CHUNK
        ts = t0 + sub * tqs
        k_start = pl.multiple_of(jnp.maximum(ts + tqs - span, 0), KEY_CHUNK)
        d = ((ts - k_start) + lax.broadcasted_iota(jnp.int32, (span, tqs), 1)
             - lax.broadcasted_iota(jnp.int32, (span, tqs), 0))
        return span, k_start, jnp.where((d >= 0) & (d < window), 0.0, NEG_INF).astype(BF16)

    def band_scores(qt, pair, geometry):
        span, k_start, bias = geometry
        kaug = jnp.concatenate([kp_ref[0, pair, pl.ds(k_start, span), :], kpos_ref[pl.ds(k_start, span), :]], axis=1)
        return biased(jnp.dot(kaug, qt, preferred_element_type=F32), bias, tqs)

    def band_output(s, vslot, geometry, sink_row=None):
        span, k_start, _ = geometry
        m = jnp.max(s, axis=0, keepdims=True)
        if sink_row is not None:
            m = jnp.maximum(m, sink_row)
        p = jnp.exp2(s - m)
        vt = load_vt(vslot, k_start // KEY_CHUNK, span // KEY_CHUNK)
        o = jnp.dot(value_rows(vt), p, preferred_element_type=F32)
        l = o[:HEAD_DIM]
        if sink_row is not None:
            l = l + jnp.exp2(sink_row - m).astype(F32)
        return o[HEAD_DIM:] / l

    tc = t0 + lax.broadcasted_iota(jnp.int32, (ncmp, tq), 1)
    cend = lax.broadcasted_iota(jnp.int32, (ncmp, tq), 0) * CMP_STRIDE + (CMP_LEN - 1)
    cmask = tc >= cend
    row_ok = (t0 + lax.broadcasted_iota(jnp.int32, (1, tq), 1)) >= (CMP_LEN - 1)
    row_ok = jnp.concatenate([row_ok.astype(F32)] * N_REP, axis=1)
    jn = lax.broadcasted_iota(jnp.int32, (nsel, ncmp), 0) * SEL_LEN
    cn = lax.broadcasted_iota(jnp.int32, (nsel, ncmp), 1) * CMP_STRIDE
    ov_t = jnp.where((cn < jn + SEL_LEN) & (cn + CMP_LEN > jn), 1.0, 0.0).astype(BF16)
    jb = lax.broadcasted_iota(jnp.int32, (nsel, tq), 0)
    cur = (t0 + lax.broadcasted_iota(jnp.int32, (nsel, tq), 1)) // SEL_LEN
    forced = (jb == 0) | (jb == cur) | (jb == cur - 1)
    valid = jb <= cur
    cmp_scores = [jnp.dot(jnp.concatenate([kvc_ref[0, g], cpos_ref[...]], axis=1), stacked_q(qn_ref, g, True, tq),
                          preferred_element_type=F32) for g in range(N_KV)]
    o_cmp, scores_sel = [], []
    for g in range(N_KV):
        s = masked(cmp_scores[g], cmask, tq)
        e = jnp.exp2(s - jnp.max(s, axis=0, keepdims=True))
        p = e * (row_ok / jnp.sum(e, axis=0, keepdims=True))
        o_cmp.append(jnp.dot(value_rows(kvct_ref[0, g, HEAD_DIM:, :]), p.astype(BF16),
                             preferred_element_type=F32)[HEAD_DIM:])
        psum = p[:, 0:tq]
        for r in range(1, N_REP):
            psum = psum + p[:, r * tq:(r + 1) * tq]
        imp_t = sum(jnp.dot(ov_t, part, preferred_element_type=F32) for part in _split3(psum))
        scores_sel.append(jnp.where(forced, SEL_BONUS, jnp.where(valid, imp_t, -1.0)))

    items = []
    for sub in range(tq // tqs):
        geo_win = band_geometry(NSA_WINDOW, sub)
        geo_swa = band_geometry(SWA_WINDOW, sub)
        for g in range(N_KV):
            sink_row = jnp.concatenate(
                [jnp.full((1, tqs), sinks_ref[g * N_REP + r] * LOG2E, F32) for r in range(N_REP)],
                axis=1).astype(BF16)
            items.append((qn_ref, g, sub, 1, N_KV + g, geo_win, None))
            items.append((qs_ref, g, sub, 2, 2 * N_KV + g, geo_swa, sink_row))

    def item_scores(item):
        q_ref, g, sub, pair, _, geo, _ = item
        return band_scores(stacked_q(q_ref, g, g == 0, tqs, sub * tqs), pair, geo)

    outs, ahead = [], 2
    pending = [item_scores(items[k]) for k in range(ahead)]
    for k, item in enumerate(items):
        if k + ahead < len(items):
            pending.append(item_scores(items[k + ahead]))
        outs.append(band_output(pending.pop(0), item[4], item[5], item[6]))

    qt_sel = []
    for score in scores_sel:
        rank = [jnp.zeros((8, tq), F32) for _ in range(nsel // 8)]
        for i in range(nsel):
            row = score[i:i + 1, :]
            for k in range(nsel // 8):
                blk = score[8 * k:8 * (k + 1)]
                ge = jnp.where(row >= blk, 1.0, 0.0)
                gt = jnp.where(row > blk, 1.0, 0.0)
                if 8 * k > i:
                    beats = ge
                elif 8 * k + 7 <= i:
                    beats = gt
                else:
                    beats = jnp.where(lax.broadcasted_iota(jnp.int32, (8, tq), 0) > i - 8 * k, ge, gt)
                rank[k] = rank[k] + beats
        sel_t = jnp.where(jnp.concatenate(rank, axis=0) < n_top, 1.0, 0.0)
        sel_t = jnp.concatenate([sel_t, jnp.zeros((LANES - nsel, tq), F32)], axis=0)
        qt_sel.append(stacked_q(qn_ref, len(qt_sel), len(qt_sel) == 0, tq, 0, sel_t))

    nck = tq // KEY_CHUNK

    def score_step(kb, n, bias=None):
        k0 = pl.multiple_of(kb * tq, tq)
        kaug = jnp.concatenate([kp_ref[0, 0, pl.ds(k0, n * tq), :], kpos_ref[pl.ds(k0, n * tq), :]], axis=1)
        raw = [jnp.dot(kaug, qt_sel[g], preferred_element_type=F32) for g in range(N_KV)]
        for g in range(N_KV):
            s = raw[g].astype(BF16) if bias is None else biased(raw[g], bias, tq)
            s_scr[g, pl.ds(kb, n)] = s.reshape(n, tq, cols)
            m_scr[g] = jnp.maximum(m_scr[g], jnp.max(s.reshape(n * tq // 16, 16, cols), axis=0))

    def value_step(kb, n):
        ps = [jnp.exp2(s_scr[g, pl.ds(kb, n)].reshape(n * tq, cols) - m_scr[g, 0:1, :]) for g in range(N_KV)]
        for g in range(N_KV):
            acc_scr[g] += jnp.dot(value_rows(load_vt(g, kb * nck, n * nck)), ps[g], preferred_element_type=F32)

    for g in range(N_KV):
        m_scr[g] = jnp.full((16, cols), NEG_INF, BF16)
        acc_scr[g] = jnp.zeros((LANES, cols), F32)

    def pass1(i, carry):
        score_step(2 * i, 2)
        return carry

    lax.fori_loop(0, qi // 2, pass1, 0)
    causal = jnp.where(lax.broadcasted_iota(jnp.int32, (tq, tq), 1) >= lax.broadcasted_iota(jnp.int32, (tq, tq), 0),
                       0.0, NEG_INF).astype(BF16)
    odd = (qi % 2) == 1

    @pl.when(odd)
    def _():
        score_step(qi - 1, 2, jnp.concatenate([jnp.zeros((tq, tq), BF16), causal], axis=0))

    @pl.when(jnp.logical_not(odd))
    def _():
        score_step(qi, 1, causal)

    for g in range(N_KV):
        m_scr[g] = jnp.broadcast_to(jnp.max(m_scr[g], axis=0, keepdims=True), (16, cols))

    def pass2(i, carry):
        value_step(2 * i, 2)
        return carry

    lax.fori_loop(0, (qi + 1) // 2, pass2, 0)

    @pl.when(jnp.logical_not(odd))
    def _():
        value_step(qi, 1)

    gsig = jax.nn.sigmoid(gn_ref[0])
    ya_cols, yb_cols = [], []
    for sub in range(tq // tqs):
        lane0 = sub * tqs
        gates = gsig[:, lane0:lane0 + tqs]
        ya_rows, yb_rows = [], []
        for g in range(N_KV):
            o_win, o_swa = outs[(sub * N_KV + g) * 2], outs[(sub * N_KV + g) * 2 + 1]
            o_slc = acc_scr[g, HEAD_DIM:, :] / acc_scr[g, :HEAD_DIM, :]
            for r in range(N_REP):
                c = (g * N_REP + r) * 3
                wide = slice(r * tq + lane0, r * tq + lane0 + tqs)
                ya_rows.append(gates[c:c + 1] * o_cmp[g][:, wide] + gates[c + 1:c + 2] * o_slc[:, wide]
                               + gates[c + 2:c + 3] * o_win[:, r * tqs:(r + 1) * tqs])
            yb_rows += [o_swa[:, r * tqs:(r + 1) * tqs] for r in range(N_REP)]
        ya_cols.append(jnp.concatenate(ya_rows, axis=0))
        yb_cols.append(jnp.concatenate(yb_rows, axis=0))
    ya_ref[0] = jnp.concatenate(ya_cols, axis=1).T.astype(BF16)
    yb_ref[0] = jnp.concatenate(yb_cols, axis=1).T.astype(BF16)


def _attn_call(sinks, qn_t, qs_t, gn_t, kp, vt, kvc, kvc_t, *, tq=256, tqs=128):
    bsz, _, s = qn_t.shape
    ncmp = kvc.shape[2]
    assert s % tq == 0 and s // SEL_LEN <= MAX_SEL_BLOCKS and s >= tq + NSA_WINDOW
    kpos, cpos = _position_tables(s)
    qspec = pl.BlockSpec((1, Q_COLS, tq), lambda b, i: (b, 0, i))
    yspec = pl.BlockSpec((1, tq, Q_COLS), lambda b, i: (b, i, 0))
    cols = N_REP * tq
    return pl.pallas_call(
        functools.partial(_attn_kernel, seq=s, tq=tq, tqs=tqs),
        grid=(bsz, s // tq),
        in_specs=[pl.BlockSpec(memory_space=pltpu.SMEM),
                  qspec, qspec,
                  pl.BlockSpec((1, GATE_ROWS, tq), lambda b, i: (b, 0, i)),
                  pl.BlockSpec((1, N_KPAIR, s, LANES), lambda b, i: (b, 0, 0, 0)),
                  pl.BlockSpec((1, N_VSLOT, s // KEY_CHUNK, HEAD_DIM, KEY_CHUNK), lambda b, i: (b, 0, 0, 0, 0)),
                  pl.BlockSpec((1, N_KV, ncmp, LANES), lambda b, i: (b, 0, 0, 0)),
                  pl.BlockSpec((1, N_KV, LANES, ncmp), lambda b, i: (b, 0, 0, 0)),
                  _const_spec((s, LANES)), _const_spec((ncmp, LANES))],
        out_specs=[yspec, yspec],
        out_shape=[jax.ShapeDtypeStruct((bsz, s, Q_COLS), BF16),
                   jax.ShapeDtypeStruct((bsz, s, Q_COLS), BF16)],
        scratch_shapes=[pltpu.VMEM((N_KV, s // tq, tq, cols), BF16),
                        pltpu.VMEM((N_KV, 16, cols), BF16),
                        pltpu.VMEM((N_KV, LANES, cols), F32)],
        compiler_params=_params(("parallel", "arbitrary")),
        name="hybrid_attention",
    )(sinks, qn_t, qs_t, gn_t, kp, vt, kvc, kvc_t, kpos, cpos)


def _merge_kernel(x_ref, sh_ref, sc_ref, gt_ref, g_ref, ya_ref, yb_ref, wgm_ref, wa_ref, wb_ref, wo_ref, o_ref, *,
                  nsplit):
    d = x_ref.shape[-1]
    rows = x_ref.shape[1] // nsplit

    def products(h):
        seg = slice(h * rows, (h + 1) * rows)
        u = _modulated_norm(x_ref[0, seg], g_ref[...], sh_ref[0], sc_ref[0]).astype(BF16)
        return (jnp.dot(ya_ref[0, seg], wa_ref[...], preferred_element_type=F32),
                jnp.dot(u, wgm_ref[:, :d], preferred_element_type=F32),
                jnp.dot(yb_ref[0, seg], wb_ref[...], preferred_element_type=F32),
                jnp.dot(u, wgm_ref[:, d:], preferred_element_type=F32))

    nxt = products(0)
    for h in range(nsplit):
        up_a, gate_a, up_b, gate_b = nxt
        if h + 1 < nsplit:
            nxt = products(h + 1)
        merged = jax.nn.sigmoid(gate_a) * up_a + jax.nn.sigmoid(gate_b) * up_b
        y = jnp.dot(merged.astype(BF16), wo_ref[...], preferred_element_type=F32)
        seg = slice(h * rows, (h + 1) * rows)
        o_ref[0, seg] = x_ref[0, seg] + gt_ref[0] * y


def _merge_call(h, shift, scale, gate, g, ya, yb, wgm, wa, wb, wo, *, tm=1024, nsplit=4):
    bsz, s, d = h.shape
    vec = pl.BlockSpec((1, 1, d), lambda b, i: (b, 0, 0))
    tok = pl.BlockSpec((1, tm, d), lambda b, i: (b, i, 0))
    ysp = pl.BlockSpec((1, tm, Q_COLS), lambda b, i: (b, i, 0))
    return pl.pallas_call(
        functools.partial(_merge_kernel, nsplit=nsplit),
        grid=(bsz, s // tm),
        in_specs=[tok, vec, vec, vec, _const_spec((1, d)), ysp, ysp,
                  _const_spec(wgm.shape), _const_spec(wa.shape), _const_spec(wb.shape), _const_spec(wo.shape)],
        out_specs=tok,
        out_shape=jax.ShapeDtypeStruct((bsz, s, d), F32),
        compiler_params=_params(("parallel", "parallel")),
        name="mixer_merge",
    )(h, shift, scale, gate, g.reshape(1, d), ya, yb, wgm, wa, wb, wo)


def _proj_column_ranges():
    kvw = 2 * N_KV * HEAD_DIM
    off_qn = 0
    off_c = off_qn + Q_COLS
    off_s = off_c + kvw
    off_w = off_s + kvw
    off_gn = off_w + kvw
    off_qs = off_gn + 3 * N_HEADS
    off_b = off_qs + Q_COLS
    off_gm = off_b + kvw
    half = N_KV * HEAD_DIM
    nn = [(off, off + half) for off in (off_s, off_w, off_b)]
    for g in range(N_KV):
        nn += [(off_c + g * HEAD_DIM, off_c + (g + 1) * HEAD_DIM),
               (off_c + half + g * HEAD_DIM, off_c + half + (g + 1) * HEAD_DIM)]
    tt = [(off_qn, off_qn + Q_COLS), (off_qs, off_qs + Q_COLS)]
    tt += [(off + half, off + 2 * half) for off in (off_s, off_w, off_b)]
    tt += [(off_gn, off_gn + 3 * N_HEADS)]
    return nn, tt, off_gm


def _take_columns(w, ranges, width):
    parts = [w[:, a:b] for a, b in ranges]
    have = sum(b - a for a, b in ranges)
    if width > have:
        parts.append(jnp.zeros((w.shape[0], width - have), w.dtype))
    return jnp.concatenate(parts, axis=1)


def _compress_weights(pos_k, w1_k, w2_k, pos_v, w1_v, w2_v):
    half = CMP_LEN // 2
    zk = jnp.zeros((half, HEAD_DIM, CMP_HIDDEN), F32)

    def w1_half(sl):
        wk = jnp.concatenate([w1_k[sl], zk], axis=-1)
        wv = jnp.concatenate([zk, w1_v[sl]], axis=-1)
        return jnp.concatenate([wk, wv], axis=1).reshape(half * 2 * HEAD_DIM, 2 * CMP_HIDDEN).astype(BF16)

    def pos_half(sl):
        p = jnp.concatenate([pos_k[sl], pos_v[sl]], axis=1).reshape(1, half * 2 * HEAD_DIM)
        return jnp.broadcast_to(p, (8, p.shape[1])).astype(BF16)

    z2 = jnp.zeros((CMP_HIDDEN, HEAD_DIM), F32)
    w2 = jnp.concatenate([jnp.concatenate([w2_k, z2], axis=1),
                          jnp.concatenate([z2, w2_v], axis=1)], axis=0).astype(BF16)
    lo, hi = slice(0, half), slice(half, CMP_LEN)
    return pos_half(lo), pos_half(hi), w1_half(lo), w1_half(hi), w2


def kernel(x, c, w_ada, b_ada, g_ffn1, w1_gate, w1_up, w1_down, g_mix, w_in, cmp_pos_k, cmp_w1_k, cmp_w2_k,
           cmp_pos_v, cmp_w1_v, cmp_w2_v, sinks, w_up_a, w_up_b, w_out, g_ffn2, w2_gate, w2_up, w2_down, g_final):
    bsz, seq, d = x.shape
    depth = w_ada.shape[0]
    nn_ranges, t_ranges, off_gm = _proj_column_ranges()
    h = x
    for l in range(depth):
        mod = _ada_call(c, w_ada[l], b_ada[l])
        sh1, sc1, gt1, sh2, sc2, gt2, sh3, sc3, gt3 = [m.reshape(bsz, 1, d) for m in jnp.split(mod, 9, axis=-1)]
        last = l == depth - 1

        h = _ffn_call(h, sh1, sc1, gt1, g_ffn1[l],
                      w1_gate[l].astype(BF16), w1_up[l].astype(BF16), w1_down[l].astype(BF16))

        w_nn = _take_columns(w_in[l], nn_ranges, NN_COLS).astype(BF16)
        w_t = _take_columns(w_in[l], t_ranges, T_ROWS).T.astype(BF16)
        kp, kvc_in, qn_t, qs_t, vt, gn_t = _proj_call(h, sh2, sc2, g_mix[l], w_nn, w_t)

        pa, pb, w1a, w1b, w2c = _compress_weights(cmp_pos_k[l], cmp_w1_k[l], cmp_w2_k[l],
                                                  cmp_pos_v[l], cmp_w1_v[l], cmp_w2_v[l])
        kvc, kvc_t = _cmp_call(kvc_in, pa, pb, w1a, w1b, w2c)

        ya, yb = _attn_call(sinks[l].reshape(-1), qn_t, qs_t, gn_t, kp, vt, kvc, kvc_t)

        h = _merge_call(h, sh2, sc2, gt2, g_mix[l], ya, yb,
                        w_in[l][:, off_gm:].astype(BF16), w_up_a[l].astype(BF16),
                        w_up_b[l].astype(BF16), w_out[l].astype(BF16))

        h = _ffn_call(h, sh3, sc3, gt3, g_ffn2[l],
                      w2_gate[l].astype(BF16), w2_up[l].astype(BF16), w2_down[l].astype(BF16),
                      g_final if last else None)
    return h
```

```python
import functools

import numpy as np
import jax
import jax.numpy as jnp
from jax import lax
from jax.experimental import pallas as pl
from jax.experimental.pallas import tpu as pltpu

F32 = jnp.float32
BF16 = jnp.bfloat16

HEAD_DIM = 64
N_HEADS = 8
N_KV = 2
N_REP = N_HEADS // N_KV
CMP_LEN = 32
CMP_STRIDE = 16
CMP_HIDDEN = 256
SEL_LEN = 64
SEL_TOPN = 8
NSA_WINDOW = 512
SWA_WINDOW = 128
FFN_RES = 0.5
RMS_EPS = 1e-6
NEG_INF = -1e30
SEL_BONUS = 1e4
ATTN_SCALE = HEAD_DIM ** -0.5

LANES = 128
VMEM_LIMIT = 56 * 1024 * 1024

SLOPES = [2.0 ** (-8.0 * (h + 1) / N_HEADS) for h in range(N_HEADS)]
LOG2E = 1.4426950408889634


def _bf16_terms(x, n=3):
    terms = []
    for _ in range(n):
        t = float(np.asarray(x, np.float32).astype(BF16).astype(np.float32))
        terms.append(t)
        x = x - t
    return terms


SLOPE_TERMS = [_bf16_terms(s * LOG2E) for s in SLOPES]


def _const_spec(shape):
    n = len(shape)
    return pl.BlockSpec(shape, lambda *_: (0,) * n, pipeline_mode=pl.Buffered(1))


def _params(sem):
    return pltpu.CompilerParams(dimension_semantics=sem, vmem_limit_bytes=VMEM_LIMIT)


def _modulated_norm(x, g, shift, scale):
    ms = jnp.mean(x * x, axis=-1, keepdims=True)
    y = x * lax.rsqrt(ms + RMS_EPS)
    return (y * g) * (1.0 + scale) + shift


def _split3(a):
    hi = a.astype(BF16)
    r1 = a - hi.astype(F32)
    mid = r1.astype(BF16)
    lo = (r1 - mid.astype(F32)).astype(BF16)
    return hi, mid, lo


def _ada_kernel(c_ref, w_ref, b_ref, o_ref):
    c = c_ref[...]
    a = c * jax.nn.sigmoid(c)
    a_hi = a.astype(BF16)
    a_lo = (a - a_hi.astype(F32)).astype(BF16)
    w = w_ref[...]
    w_hi = w.astype(BF16)
    w_lo = (w - w_hi.astype(F32)).astype(BF16)
    acc = jnp.dot(a_hi, w_hi, preferred_element_type=F32)
    acc += jnp.dot(a_hi, w_lo, preferred_element_type=F32)
    acc += jnp.dot(a_lo, w_hi, preferred_element_type=F32)
    o_ref[...] = acc + b_ref[...]


def _ada_call(c, w, b):
    bsz, d = c.shape
    n = w.shape[1]
    tn = 1024
    return pl.pallas_call(
        _ada_kernel,
        grid=(n // tn,),
        in_specs=[pl.BlockSpec((bsz, d), lambda j: (0, 0)),
                  pl.BlockSpec((d, tn), lambda j: (0, j)),
                  pl.BlockSpec((1, tn), lambda j: (0, j))],
        out_specs=pl.BlockSpec((bsz, tn), lambda j: (0, j)),
        out_shape=jax.ShapeDtypeStruct((bsz, n), F32),
        compiler_params=_params(("parallel",)),
        name="adaln",
    )(c, w, b.reshape(1, n))


def _ffn_kernel(x_ref, sh_ref, sc_ref, gt_ref, g_ref, wg_ref, wu_ref, wd_ref, *rest, tf, final, nsplit):
    o_ref = rest[-1]
    rows = x_ref.shape[1] // nsplit
    dff = wg_ref.shape[1]
    for h in range(nsplit):
        seg = slice(h * rows, (h + 1) * rows)
        x = x_ref[0, seg]
        u = _modulated_norm(x, g_ref[...], sh_ref[0], sc_ref[0]).astype(BF16)
        acc = None
        for c in range(dff // tf):
            cols = slice(c * tf, (c + 1) * tf)
            gate = jnp.dot(u, wg_ref[:, cols], preferred_element_type=F32)
            up = jnp.dot(u, wu_ref[:, cols], preferred_element_type=F32)
            act = (gate * jax.nn.sigmoid(gate) * up).astype(BF16)
            part = jnp.dot(act, wd_ref[cols, :], preferred_element_type=F32)
            acc = part if acc is None else acc + part
        y = x + (FFN_RES * gt_ref[0]) * acc
        if final:
            gfin_ref = rest[0]
            ms = jnp.mean(y * y, axis=-1, keepdims=True)
            y = (y * lax.rsqrt(ms + RMS_EPS)) * gfin_ref[...]
        o_ref[0, seg] = y


def _ffn_call(h, shift, scale, gate, g, wg, wu, wd, g_final=None, *, tm=1024, tf=256, nsplit=2):
    bsz, s, d = h.shape
    dff = wg.shape[1]
    final = g_final is not None
    vec = pl.BlockSpec((1, 1, d), lambda b, i: (b, 0, 0))
    in_specs = [pl.BlockSpec((1, tm, d), lambda b, i: (b, i, 0)), vec, vec, vec,
                _const_spec((1, d)), _const_spec((d, dff)), _const_spec((d, dff)), _const_spec((dff, d))]
    args = [h, shift, scale, gate, g.reshape(1, d), wg, wu, wd]
    if final:
        in_specs.append(_const_spec((1, d)))
        args.append(g_final.reshape(1, d))
    return pl.pallas_call(
        functools.partial(_ffn_kernel, tf=tf, final=final, nsplit=nsplit),
        grid=(bsz, s // tm),
        in_specs=in_specs,
        out_specs=pl.BlockSpec((1, tm, d), lambda b, i: (b, i, 0)),
        out_shape=jax.ShapeDtypeStruct((bsz, s, d), F32),
        compiler_params=_params(("parallel", "parallel")),
        name="ffn_final" if final else "ffn",
    )(*args)


N_KPAIR = 3
Q_COLS = N_HEADS * HEAD_DIM
NN_COLS = (N_KPAIR + N_KV) * LANES
N_VSLOT = N_KPAIR * N_KV
GATE_ROWS = 32
VT_ROWS = N_VSLOT * HEAD_DIM
T_ROWS = 2 * Q_COLS + VT_ROWS + GATE_ROWS
KEY_CHUNK = LANES


def _proj_kernel(x_ref, sh_ref, sc_ref, g_ref, wn_ref, wt_ref, kp_ref, kvc_ref, qn_ref, qs_ref, vt_ref, gn_ref,
                 slab_scr, *, nsplit):
    rows = x_ref.shape[1] // nsplit

    def products(h):
        u = _modulated_norm(x_ref[0, h * rows:(h + 1) * rows], g_ref[...], sh_ref[0], sc_ref[0]).astype(BF16)
        return (jnp.dot(u, wn_ref[...], preferred_element_type=F32),
                lax.dot_general(wt_ref[...], u, (((1,), (1,)), ((), ())), preferred_element_type=F32))

    nxt = products(0)
    for h in range(nsplit):
        nn, tt = nxt
        if h + 1 < nsplit:
            nxt = products(h + 1)
        seg = slice(h * rows, (h + 1) * rows)
        for i in range(N_KPAIR):
            kp_ref[0, i, seg] = nn[:, i * LANES:(i + 1) * LANES].astype(BF16)
        crow = slice(h * rows // CMP_STRIDE, (h + 1) * rows // CMP_STRIDE)
        for i in range(N_KV):
            slab_scr[...] = nn[:, (N_KPAIR + i) * LANES:(N_KPAIR + i + 1) * LANES]
            for t in range(CMP_STRIDE):
                part = slab_scr[pl.ds(t, rows // CMP_STRIDE, stride=CMP_STRIDE), :]
                kvc_ref[0, i, crow, t * LANES:(t + 1) * LANES] = part.astype(BF16)
        qn_ref[0, :, seg] = (tt[0:Q_COLS] * (ATTN_SCALE * LOG2E)).astype(BF16)
        qs_ref[0, :, seg] = (tt[Q_COLS:2 * Q_COLS] * (ATTN_SCALE * LOG2E)).astype(BF16)
        base = 2 * Q_COLS
        for s in range(N_VSLOT):
            for c in range(rows // KEY_CHUNK):
                vt_ref[0, s, h * (rows // KEY_CHUNK) + c] = tt[base + s * HEAD_DIM: base + (s + 1) * HEAD_DIM,
                                                               c * KEY_CHUNK:(c + 1) * KEY_CHUNK].astype(BF16)
        gn_ref[0, :, seg] = tt[base + VT_ROWS:]


def _proj_call(h, shift, scale, g, wn, wt, *, tm=1024, nsplit=2):
    bsz, s, d = h.shape
    vec = pl.BlockSpec((1, 1, d), lambda b, i: (b, 0, 0))
    nck = tm // KEY_CHUNK
    return pl.pallas_call(
        functools.partial(_proj_kernel, nsplit=nsplit),
        grid=(bsz, s // tm),
        in_specs=[pl.BlockSpec((1, tm, d), lambda b, i: (b, i, 0)), vec, vec,
                  _const_spec((1, d)), _const_spec((d, NN_COLS)), _const_spec((T_ROWS, d))],
        out_specs=[pl.BlockSpec((1, N_KPAIR, tm, LANES), lambda b, i: (b, 0, i, 0)),
                   pl.BlockSpec((1, N_KV, tm // CMP_STRIDE, CMP_STRIDE * LANES), lambda b, i: (b, 0, i, 0)),
                   pl.BlockSpec((1, Q_COLS, tm), lambda b, i: (b, 0, i)),
                   pl.BlockSpec((1, Q_COLS, tm), lambda b, i: (b, 0, i)),
                   pl.BlockSpec((1, N_VSLOT, nck, HEAD_DIM, KEY_CHUNK), lambda b, i: (b, 0, i, 0, 0)),
                   pl.BlockSpec((1, GATE_ROWS, tm), lambda b, i: (b, 0, i))],
        out_shape=[jax.ShapeDtypeStruct((bsz, N_KPAIR, s, LANES), BF16),
                   jax.ShapeDtypeStruct((bsz, N_KV, s // CMP_STRIDE, CMP_STRIDE * LANES), BF16),
                   jax.ShapeDtypeStruct((bsz, Q_COLS, s), BF16),
                   jax.ShapeDtypeStruct((bsz, Q_COLS, s), BF16),
                   jax.ShapeDtypeStruct((bsz, N_VSLOT, s // KEY_CHUNK, HEAD_DIM, KEY_CHUNK), BF16),
                   jax.ShapeDtypeStruct((bsz, GATE_ROWS, s), F32)],
        scratch_shapes=[pltpu.VMEM((tm // nsplit, LANES), F32)],
        compiler_params=_params(("parallel", "parallel")),
        name="mixer_proj",
    )(h, shift, scale, g.reshape(1, d), wn, wt)


def _cmp_kernel(a_ref, pa_ref, pb_ref, w1a_ref, w1b_ref, w2_ref, o_ref, ot_ref):
    nb, ng, n, _ = a_ref.shape
    bias = (jnp.dot(pa_ref[...], w1a_ref[...], preferred_element_type=F32)
            + jnp.dot(pb_ref[...], w1b_ref[...], preferred_element_type=F32))[0:1]
    a = a_ref[...].reshape(nb * ng * n, a_ref.shape[3])
    first = jnp.dot(a, w1a_ref[...], preferred_element_type=F32)
    second = jnp.dot(a, w1b_ref[...], preferred_element_type=F32)
    for i in range(nb * ng):
        rows = slice(i * n, (i + 1) * n)
        hid = first[rows] + pltpu.roll(second[rows], n - 1, axis=0) + bias
        hid = jax.nn.gelu(hid)
        out = jnp.dot(hid.astype(BF16), w2_ref[...], preferred_element_type=F32)
        o_ref[i // ng, i % ng] = out.astype(BF16)
        ot_ref[i // ng, i % ng] = out.T.astype(BF16)


def _cmp_call(kv_chunks, pa, pb, w1a, w1b, w2):
    bsz, _, nchunk, width = kv_chunks.shape
    nb = 2 if bsz % 2 == 0 else 1
    return pl.pallas_call(
        _cmp_kernel,
        grid=(bsz // nb,),
        in_specs=[pl.BlockSpec((nb, N_KV, nchunk, width), lambda b: (b, 0, 0, 0)),
                  _const_spec(pa.shape), _const_spec(pb.shape),
                  _const_spec(w1a.shape), _const_spec(w1b.shape), _const_spec(w2.shape)],
        out_specs=[pl.BlockSpec((nb, N_KV, nchunk, LANES), lambda b: (b, 0, 0, 0)),
                   pl.BlockSpec((nb, N_KV, LANES, nchunk), lambda b: (b, 0, 0, 0))],
        out_shape=[jax.ShapeDtypeStruct((bsz, N_KV, nchunk, LANES), BF16),
                   jax.ShapeDtypeStruct((bsz, N_KV, LANES, nchunk), BF16)],
        compiler_params=_params(("parallel",)),
        name="nsa_compress",
    )(kv_chunks, pa, pb, w1a, w1b, w2)


POS_HI, POS_LO, POS_ONE = 96, 99, 102
MAX_SEL_BLOCKS = POS_HI


def _position_tables(seq):
    key = np.arange(seq)
    kpos = np.zeros((seq, LANES), np.float32)
    kpos[key, key // SEL_LEN] = 1.0
    kpos[:, POS_HI:POS_HI + 3] = ((key // SEL_LEN) * SEL_LEN)[:, None]
    kpos[:, POS_LO:POS_LO + 3] = (key % SEL_LEN)[:, None]
    kpos[:, POS_ONE:POS_ONE + 3] = 1.0
    ncmp = seq // CMP_STRIDE
    cpos = np.zeros((ncmp, LANES), np.float32)
    cpos[:, POS_HI:POS_HI + 3] = (np.arange(ncmp) * CMP_STRIDE)[:, None]
    cpos[:, POS_LO:POS_LO + 3] = CMP_LEN - 1
    cpos[:, POS_ONE:POS_ONE + 3] = 1.0
    return jnp.asarray(kpos, BF16), jnp.asarray(cpos, BF16)


def _attn_kernel(sinks_ref, qn_ref, qs_ref, gn_ref, kp_ref, vt_ref, kvc_ref, kvct_ref, kpos_ref, cpos_ref,
                 ya_ref, yb_ref, s_scr, m_scr, acc_scr, blocks_ref, *, seq, tq, tqs):
    qi = pl.program_id(1)
    t0 = qi * tq
    cols = N_REP * tq
    nsel = seq // SEL_LEN
    ncmp = kvc_ref.shape[2]
    n_top = min(SEL_TOPN, nsel)

    def stacked_q(q_ref, g, low, width, lane0=0, sel_t=None):
        sub = lax.broadcasted_iota(jnp.int32, (LANES, width), 0)
        zero_q = jnp.zeros((HEAD_DIM, width), BF16)
        blocks = []
        for r in range(N_REP):
            h = g * N_REP + r
            q = q_ref[0, h * HEAD_DIM:(h + 1) * HEAD_DIM, lane0:lane0 + width]
            qpart = jnp.concatenate([q, zero_q] if low else [zero_q, q], axis=0)
            ext = jnp.zeros((LANES, width), F32)
            for i, c in enumerate(SLOPE_TERMS[h]):
                ext = jnp.where((sub == POS_HI + i) | (sub == POS_LO + i), c, ext)
            tpos = (t0 + lane0 + lax.broadcasted_iota(jnp.int32, (1, width), 1)).astype(F32)
            for i, part in enumerate(_split3(-(SLOPES[h] * LOG2E) * tpos)):
                ext = jnp.where(sub == POS_ONE + i, part.astype(F32), ext)
            if sel_t is not None:
                ext = ext + jnp.where(sub < nsel, (sel_t - 1.0) * (-NEG_INF), 0.0)
            blocks.append(jnp.concatenate([qpart, ext.astype(BF16)], axis=0))
        return jnp.concatenate(blocks, axis=1)

    def masked(s, mask, width):
        return jnp.concatenate(
            [jnp.where(mask, s[:, r * width:(r + 1) * width], NEG_INF) for r in range(N_REP)], axis=1)

    def biased(s, bias, width):
        sb = s.astype(BF16)
        return jnp.concatenate([sb[:, r * width:(r + 1) * width] + bias for r in range(N_REP)], axis=1)

    def value_rows(vt):
        return jnp.concatenate([jnp.ones_like(vt), vt], axis=0)

    def load_vt(slot, first_chunk, nchunks):
        return jnp.concatenate([vt_ref[0, slot, first_chunk + j] for j in range(nchunks)], axis=1)

    def band_geometry(window, sub):
        span = tqs + -(-window // KEY_CHUNK) * KEY_CHUNK
        ts = t0 + sub * tqs
        k_start = pl.multiple_of(jnp.maximum(ts + tqs - span, 0), KEY_CHUNK)
        d = ((ts - k_start) + lax.broadcasted_iota(jnp.int32, (span, tqs), 1)
             - lax.broadcasted_iota(jnp.int32, (span, tqs), 0))
        return span, k_start, jnp.where((d >= 0) & (d < window), 0.0, NEG_INF).astype(BF16)

    def band_scores(qt, pair, geometry):
        span, k_start, bias = geometry
        kaug = jnp.concatenate([kp_ref[0, pair, pl.ds(k_start, span), :], kpos_ref[pl.ds(k_start, span), :]], axis=1)
        return biased(jnp.dot(kaug, qt, preferred_element_type=F32), bias, tqs)

    def band_output(s, vslot, geometry, sink_row=None):
        span, k_start, _ = geometry
        m = jnp.max(s, axis=0, keepdims=True)
        if sink_row is not None:
            m = jnp.maximum(m, sink_row)
        p = jnp.exp2(s - m)
        vt = load_vt(vslot, k_start // KEY_CHUNK, span // KEY_CHUNK)
        o = jnp.dot(value_rows(vt), p, preferred_element_type=F32)
        l = o[:HEAD_DIM]
        if sink_row is not None:
            l = l + jnp.exp2(sink_row - m).astype(F32)
        return o[HEAD_DIM:] / l

    tc = t0 + lax.broadcasted_iota(jnp.int32, (ncmp, tq), 1)
    cend = lax.broadcasted_iota(jnp.int32, (ncmp, tq), 0) * CMP_STRIDE + (CMP_LEN - 1)
    cmask = tc >= cend
    row_ok = (t0 + lax.broadcasted_iota(jnp.int32, (1, tq), 1)) >= (CMP_LEN - 1)
    row_ok = jnp.concatenate([row_ok.astype(F32)] * N_REP, axis=1)
    jn = lax.broadcasted_iota(jnp.int32, (nsel, ncmp), 0) * SEL_LEN
    cn = lax.broadcasted_iota(jnp.int32, (nsel, ncmp), 1) * CMP_STRIDE
    ov_t = jnp.where((cn < jn + SEL_LEN) & (cn + CMP_LEN > jn), 1.0, 0.0).astype(BF16)
    jb = lax.broadcasted_iota(jnp.int32, (nsel, tq), 0)
    cur = (t0 + lax.broadcasted_iota(jnp.int32, (nsel, tq), 1)) // SEL_LEN
    forced = (jb == 0) | (jb == cur) | (jb == cur - 1)
    valid = jb <= cur
    cmp_scores = [jnp.dot(jnp.concatenate([kvc_ref[0, g], cpos_ref[...]], axis=1), stacked_q(qn_ref, g, True, tq),
                          preferred_element_type=F32) for g in range(N_KV)]
    o_cmp, scores_sel = [], []
    for g in range(N_KV):
        s = masked(cmp_scores[g], cmask, tq)
        e = jnp.exp2(s - jnp.max(s, axis=0, keepdims=True))
        p = e * (row_ok / jnp.sum(e, axis=0, keepdims=True))
        o_cmp.append(jnp.dot(value_rows(kvct_ref[0, g, HEAD_DIM:, :]), p.astype(BF16),
                             preferred_element_type=F32)[HEAD_DIM:])
        psum = p[:, 0:tq]
        for r in range(1, N_REP):
            psum = psum + p[:, r * tq:(r + 1) * tq]
        imp_t = sum(jnp.dot(ov_t, part, preferred_element_type=F32) for part in _split3(psum))
        scores_sel.append(jnp.where(forced, SEL_BONUS, jnp.where(valid, imp_t, -1.0)))

    items = []
    for sub in range(tq // tqs):
        geo_win = band_geometry(NSA_WINDOW, sub)
        geo_swa = band_geometry(SWA_WINDOW, sub)
        for g in range(N_KV):
            sink_row = jnp.concatenate(
                [jnp.full((1, tqs), sinks_ref[g * N_REP + r] * LOG2E, F32) for r in range(N_REP)],
                axis=1).astype(BF16)
            items.append((qn_ref, g, sub, 1, N_KV + g, geo_win, None))
            items.append((qs_ref, g, sub, 2, 2 * N_KV + g, geo_swa, sink_row))

    def item_scores(item):
        q_ref, g, sub, pair, _, geo, _ = item
        return band_scores(stacked_q(q_ref, g, g == 0, tqs, sub * tqs), pair, geo)

    outs, ahead = [], 2
    pending = [item_scores(items[k]) for k in range(ahead)]
    for k, item in enumerate(items):
        if k + ahead < len(items):
            pending.append(item_scores(items[k + ahead]))
        outs.append(band_output(pending.pop(0), item[4], item[5], item[6]))

    qt_sel, chosen = [], None
    for score in scores_sel:
        rank = [jnp.zeros((8, tq), F32) for _ in range(nsel // 8)]
        for i in range(nsel):
            row = score[i:i + 1, :]
            for k in range(nsel // 8):
                blk = score[8 * k:8 * (k + 1)]
                ge = jnp.where(row >= blk, 1.0, 0.0)
                gt = jnp.where(row > blk, 1.0, 0.0)
                if 8 * k > i:
                    beats = ge
                elif 8 * k + 7 <= i:
                    beats = gt
                else:
                    beats = jnp.where(lax.broadcasted_iota(jnp.int32, (8, tq), 0) > i - 8 * k, ge, gt)
                rank[k] = rank[k] + beats
        sel_t = jnp.where(jnp.concatenate(rank, axis=0) < n_top, 1.0, 0.0)
        sel_t = jnp.concatenate([sel_t, jnp.zeros((LANES - nsel, tq), F32)], axis=0)
        chosen = sel_t if chosen is None else chosen + sel_t
        qt_sel.append(stacked_q(qn_ref, len(qt_sel), len(qt_sel) == 0, tq, 0, sel_t))

    nck = tq // KEY_CHUNK
    per_blk = tq // SEL_LEN
    n_prev = jnp.int32(0)
    for kb in range(seq // tq - 1):
        needed = (jnp.max(chosen[kb * per_blk:(kb + 1) * per_blk, :]) > 0.5) & (kb < qi)
        blocks_ref[n_prev] = kb
        n_prev = n_prev + needed.astype(jnp.int32)
    blocks_ref[n_prev] = qi

    def key_rows(ids):
        starts = [pl.multiple_of(kb * tq, tq) for kb in ids]
        return jnp.concatenate([jnp.concatenate([kp_ref[0, 0, pl.ds(k0, tq), :] for k0 in starts], axis=0),
                                jnp.concatenate([kpos_ref[pl.ds(k0, tq), :] for k0 in starts], axis=0)], axis=1)

    def score_step(ids, slot, bias=None):
        n = len(ids)
        kaug = key_rows(ids)
        raw = [jnp.dot(kaug, qt_sel[g], preferred_element_type=F32) for g in range(N_KV)]
        for g in range(N_KV):
            s = raw[g].astype(BF16) if bias is None else biased(raw[g], bias, tq)
            s_scr[g, pl.ds(slot, n)] = s.reshape(n, tq, cols)
            m_scr[g] = jnp.maximum(m_scr[g], jnp.max(s.reshape(n * tq // 16, 16, cols), axis=0))

    def value_step(ids, slot):
        n = len(ids)
        ps = [jnp.exp2(s_scr[g, pl.ds(slot, n)].reshape(n * tq, cols) - m_scr[g, 0:1, :]) for g in range(N_KV)]
        for g in range(N_KV):
            vt = jnp.concatenate([load_vt(g, kb * nck, nck) for kb in ids], axis=1)
            acc_scr[g] += jnp.dot(value_rows(vt), ps[g], preferred_element_type=F32)

    for g in range(N_KV):
        m_scr[g] = jnp.full((16, cols), NEG_INF, BF16)
        acc_scr[g] = jnp.zeros((LANES, cols), F32)

    def pass1(i, carry):
        score_step((blocks_ref[2 * i], blocks_ref[2 * i + 1]), 2 * i)
        return carry

    lax.fori_loop(0, n_prev // 2, pass1, 0)
    causal = jnp.where(lax.broadcasted_iota(jnp.int32, (tq, tq), 1) >= lax.broadcasted_iota(jnp.int32, (tq, tq), 0),
                       0.0, NEG_INF).astype(BF16)
    odd = (n_prev % 2) == 1

    @pl.when(odd)
    def _():
        score_step((blocks_ref[n_prev - 1], qi), n_prev - 1,
                   jnp.concatenate([jnp.zeros((tq, tq), BF16), causal], axis=0))

    @pl.when(jnp.logical_not(odd))
    def _():
        score_step((qi,), n_prev, causal)

    for g in range(N_KV):
        m_scr[g] = jnp.broadcast_to(jnp.max(m_scr[g], axis=0, keepdims=True), (16, cols))

    def pass2(i, carry):
        value_step((blocks_ref[2 * i], blocks_ref[2 * i + 1]), 2 * i)
        return carry

    lax.fori_loop(0, (n_prev + 1) // 2, pass2, 0)

    @pl.when(jnp.logical_not(odd))
    def _():
        value_step((qi,), n_prev)

    gsig = jax.nn.sigmoid(gn_ref[0])
    ya_cols, yb_cols = [], []
    for sub in range(tq // tqs):
        lane0 = sub * tqs
        gates = gsig[:, lane0:lane0 + tqs]
        ya_rows, yb_rows = [], []
        for g in range(N_KV):
            o_win, o_swa = outs[(sub * N_KV + g) * 2], outs[(sub * N_KV + g) * 2 + 1]
            o_slc = acc_scr[g, HEAD_DIM:, :] / acc_scr[g, :HEAD_DIM, :]
            for r in range(N_REP):
                c = (g * N_REP + r) * 3
                wide = slice(r * tq + lane0, r * tq + lane0 + tqs)
                ya_rows.append(gates[c:c + 1] * o_cmp[g][:, wide] + gates[c + 1:c + 2] * o_slc[:, wide]
                               + gates[c + 2:c + 3] * o_win[:, r * tqs:(r + 1) * tqs])
            yb_rows += [o_swa[:, r * tqs:(r + 1) * tqs] for r in range(N_REP)]
        ya_cols.append(jnp.concatenate(ya_rows, axis=0))
        yb_cols.append(jnp.concatenate(yb_rows, axis=0))
    ya_ref[0] = jnp.concatenate(ya_cols, axis=1).T.astype(BF16)
    yb_ref[0] = jnp.concatenate(yb_cols, axis=1).T.astype(BF16)


def _attn_call(sinks, qn_t, qs_t, gn_t, kp, vt, kvc, kvc_t, *, tq=256, tqs=128):
    bsz, _, s = qn_t.shape
    ncmp = kvc.shape[2]
    assert s % tq == 0 and s // SEL_LEN <= MAX_SEL_BLOCKS and s >= tq + NSA_WINDOW
    kpos, cpos = _position_tables(s)
    qspec = pl.BlockSpec((1, Q_COLS, tq), lambda b, i: (b, 0, i))
    yspec = pl.BlockSpec((1, tq, Q_COLS), lambda b, i: (b, i, 0))
    cols = N_REP * tq
    return pl.pallas_call(
        functools.partial(_attn_kernel, seq=s, tq=tq, tqs=tqs),
        grid=(bsz, s // tq),
        in_specs=[pl.BlockSpec(memory_space=pltpu.SMEM),
                  qspec, qspec,
                  pl.BlockSpec((1, GATE_ROWS, tq), lambda b, i: (b, 0, i)),
                  pl.BlockSpec((1, N_KPAIR, s, LANES), lambda b, i: (b, 0, 0, 0)),
                  pl.BlockSpec((1, N_VSLOT, s // KEY_CHUNK, HEAD_DIM, KEY_CHUNK), lambda b, i: (b, 0, 0, 0, 0)),
                  pl.BlockSpec((1, N_KV, ncmp, LANES), lambda b, i: (b, 0, 0, 0)),
                  pl.BlockSpec((1, N_KV, LANES, ncmp), lambda b, i: (b, 0, 0, 0)),
                  _const_spec((s, LANES)), _const_spec((ncmp, LANES))],
        out_specs=[yspec, yspec],
        out_shape=[jax.ShapeDtypeStruct((bsz, s, Q_COLS), BF16),
                   jax.ShapeDtypeStruct((bsz, s, Q_COLS), BF16)],
        scratch_shapes=[pltpu.VMEM((N_KV, s // tq, tq, cols), BF16),
                        pltpu.VMEM((N_KV, 16, cols), BF16),
                        pltpu.VMEM((N_KV, LANES, cols), F32),
                        pltpu.SMEM((s // tq + 1,), jnp.int32)],
        compiler_params=_params(("parallel", "arbitrary")),
        name="hybrid_attention",
    )(sinks, qn_t, qs_t, gn_t, kp, vt, kvc, kvc_t, kpos, cpos)


def _merge_kernel(x_ref, sh_ref, sc_ref, gt_ref, g_ref, ya_ref, yb_ref, wgm_ref, wa_ref, wb_ref, wo_ref, o_ref, *,
                  nsplit):
    d = x_ref.shape[-1]
    rows = x_ref.shape[1] // nsplit

    def products(h):
        seg = slice(h * rows, (h + 1) * rows)
        u = _modulated_norm(x_ref[0, seg], g_ref[...], sh_ref[0], sc_ref[0]).astype(BF16)
        return (jnp.dot(ya_ref[0, seg], wa_ref[...], preferred_element_type=F32),
                jnp.dot(u, wgm_ref[:, :d], preferred_element_type=F32),
                jnp.dot(yb_ref[0, seg], wb_ref[...], preferred_element_type=F32),
                jnp.dot(u, wgm_ref[:, d:], preferred_element_type=F32))

    nxt = products(0)
    for h in range(nsplit):
        up_a, gate_a, up_b, gate_b = nxt
        if h + 1 < nsplit:
            nxt = products(h + 1)
        merged = jax.nn.sigmoid(gate_a) * up_a + jax.nn.sigmoid(gate_b) * up_b
        y = jnp.dot(merged.astype(BF16), wo_ref[...], preferred_element_type=F32)
        seg = slice(h * rows, (h + 1) * rows)
        o_ref[0, seg] = x_ref[0, seg] + gt_ref[0] * y


def _merge_call(h, shift, scale, gate, g, ya, yb, wgm, wa, wb, wo, *, tm=1024, nsplit=4):
    bsz, s, d = h.shape
    vec = pl.BlockSpec((1, 1, d), lambda b, i: (b, 0, 0))
    tok = pl.BlockSpec((1, tm, d), lambda b, i: (b, i, 0))
    ysp = pl.BlockSpec((1, tm, Q_COLS), lambda b, i: (b, i, 0))
    return pl.pallas_call(
        functools.partial(_merge_kernel, nsplit=nsplit),
        grid=(bsz, s // tm),
        in_specs=[tok, vec, vec, vec, _const_spec((1, d)), ysp, ysp,
                  _const_spec(wgm.shape), _const_spec(wa.shape), _const_spec(wb.shape), _const_spec(wo.shape)],
        out_specs=tok,
        out_shape=jax.ShapeDtypeStruct((bsz, s, d), F32),
        compiler_params=_params(("parallel", "parallel")),
        name="mixer_merge",
    )(h, shift, scale, gate, g.reshape(1, d), ya, yb, wgm, wa, wb, wo)


def _proj_column_ranges():
    kvw = 2 * N_KV * HEAD_DIM
    off_qn = 0
    off_c = off_qn + Q_COLS
    off_s = off_c + kvw
    off_w = off_s + kvw
    off_gn = off_w + kvw
    off_qs = off_gn + 3 * N_HEADS
    off_b = off_qs + Q_COLS
    off_gm = off_b + kvw
    half = N_KV * HEAD_DIM
    nn = [(off, off + half) for off in (off_s, off_w, off_b)]
    for g in range(N_KV):
        nn += [(off_c + g * HEAD_DIM, off_c + (g + 1) * HEAD_DIM),
               (off_c + half + g * HEAD_DIM, off_c + half + (g + 1) * HEAD_DIM)]
    tt = [(off_qn, off_qn + Q_COLS), (off_qs, off_qs + Q_COLS)]
    tt += [(off + half, off + 2 * half) for off in (off_s, off_w, off_b)]
    tt += [(off_gn, off_gn + 3 * N_HEADS)]
    return nn, tt, off_gm


def _take_columns(w, ranges, width):
    parts = [w[:, a:b] for a, b in ranges]
    have = sum(b - a for a, b in ranges)
    if width > have:
        parts.append(jnp.zeros((w.shape[0], width - have), w.dtype))
    return jnp.concatenate(parts, axis=1)


def _compress_weights(pos_k, w1_k, w2_k, pos_v, w1_v, w2_v):
    half = CMP_LEN // 2
    zk = jnp.zeros((half, HEAD_DIM, CMP_HIDDEN), F32)

    def w1_half(sl):
        wk = jnp.concatenate([w1_k[sl], zk], axis=-1)
        wv = jnp.concatenate([zk, w1_v[sl]], axis=-1)
        return jnp.concatenate([wk, wv], axis=1).reshape(half * 2 * HEAD_DIM, 2 * CMP_HIDDEN).astype(BF16)

    def pos_half(sl):
        p = jnp.concatenate([pos_k[sl], pos_v[sl]], axis=1).reshape(1, half * 2 * HEAD_DIM)
        return jnp.broadcast_to(p, (8, p.shape[1])).astype(BF16)

    z2 = jnp.zeros((CMP_HIDDEN, HEAD_DIM), F32)
    w2 = jnp.concatenate([jnp.concatenate([w2_k, z2], axis=1),
                          jnp.concatenate([z2, w2_v], axis=1)], axis=0).astype(BF16)
    lo, hi = slice(0, half), slice(half, CMP_LEN)
    return pos_half(lo), pos_half(hi), w1_half(lo), w1_half(hi), w2


def kernel(x, c, w_ada, b_ada, g_ffn1, w1_gate, w1_up, w1_down, g_mix, w_in, cmp_pos_k, cmp_w1_k, cmp_w2_k,
           cmp_pos_v, cmp_w1_v, cmp_w2_v, sinks, w_up_a, w_up_b, w_out, g_ffn2, w2_gate, w2_up, w2_down, g_final):
    bsz, seq, d = x.shape
    depth = w_ada.shape[0]
    nn_ranges, t_ranges, off_gm = _proj_column_ranges()
    h = x
    for l in range(depth):
        mod = _ada_call(c, w_ada[l], b_ada[l])
        sh1, sc1, gt1, sh2, sc2, gt2, sh3, sc3, gt3 = [m.reshape(bsz, 1, d) for m in jnp.split(mod, 9, axis=-1)]
        last = l == depth - 1

        h = _ffn_call(h, sh1, sc1, gt1, g_ffn1[l],
                      w1_gate[l].astype(BF16), w1_up[l].astype(BF16), w1_down[l].astype(BF16))

        w_nn = _take_columns(w_in[l], nn_ranges, NN_COLS).astype(BF16)
        w_t = _take_columns(w_in[l], t_ranges, T_ROWS).T.astype(BF16)
        kp, kvc_in, qn_t, qs_t, vt, gn_t = _proj_call(h, sh2, sc2, g_mix[l], w_nn, w_t)

        pa, pb, w1a, w1b, w2c = _compress_weights(cmp_pos_k[l], cmp_w1_k[l], cmp_w2_k[l],
                                                  cmp_pos_v[l], cmp_w1_v[l], cmp_w2_v[l])
        kvc, kvc_t = _cmp_call(kvc_in, pa, pb, w1a, w1b, w2c)

        ya, yb = _attn_call(sinks[l].reshape(-1), qn_t, qs_t, gn_t, kp, vt, kvc, kvc_t)

        h = _merge_call(h, sh2, sc2, gt2, g_mix[l], ya, yb,
                        w_in[l][:, off_gm:].astype(BF16), w_up_a[l].astype(BF16),
                        w_up_b[l].astype(BF16), w_out[l].astype(BF16))

        h = _ffn_call(h, sh3, sc3, gt3, g_ffn2[l],
                      w2_gate[l].astype(BF16), w2_up[l].astype(BF16), w2_down[l].astype(BF16),
                      g_final if last else None)
    return h
```

```python
import functools

import numpy as np
import jax
import jax.numpy as jnp
from jax import lax
from jax.experimental import pallas as pl
from jax.experimental.pallas import tpu as pltpu

F32 = jnp.float32
BF16 = jnp.bfloat16

HEAD_DIM = 64
N_HEADS = 8
N_KV = 2
N_REP = N_HEADS // N_KV
CMP_LEN = 32
CMP_STRIDE = 16
CMP_HIDDEN = 256
SEL_LEN = 64
SEL_TOPN = 8
NSA_WINDOW = 512
SWA_WINDOW = 128
FFN_RES = 0.5
RMS_EPS = 1e-6
NEG_INF = -1e30
SEL_BONUS = 1e4
ATTN_SCALE = HEAD_DIM ** -0.5

LANES = 128
VMEM_LIMIT = 56 * 1024 * 1024

SLOPES = [2.0 ** (-8.0 * (h + 1) / N_HEADS) for h in range(N_HEADS)]
LOG2E = 1.4426950408889634


def _bf16_terms(x, n=3):
    terms = []
    for _ in range(n):
        t = float(np.asarray(x, np.float32).astype(BF16).astype(np.float32))
        terms.append(t)
        x = x - t
    return terms


SLOPE_TERMS = [_bf16_terms(s * LOG2E) for s in SLOPES]


def _const_spec(shape):
    n = len(shape)
    return pl.BlockSpec(shape, lambda *_: (0,) * n, pipeline_mode=pl.Buffered(1))


def _params(sem):
    return pltpu.CompilerParams(dimension_semantics=sem, vmem_limit_bytes=VMEM_LIMIT)


def _modulated_norm(x, g, shift, scale):
    ms = jnp.mean(x * x, axis=-1, keepdims=True)
    y = x * lax.rsqrt(ms + RMS_EPS)
    return (y * g) * (1.0 + scale) + shift


def _split3(a):
    hi = a.astype(BF16)
    r1 = a - hi.astype(F32)
    mid = r1.astype(BF16)
    lo = (r1 - mid.astype(F32)).astype(BF16)
    return hi, mid, lo


def _ada_kernel(c_ref, w_ref, b_ref, o_ref):
    c = c_ref[...]
    a = c * jax.nn.sigmoid(c)
    a_hi = a.astype(BF16)
    a_lo = (a - a_hi.astype(F32)).astype(BF16)
    w = w_ref[...]
    w_hi = w.astype(BF16)
    w_lo = (w - w_hi.astype(F32)).astype(BF16)
    acc = jnp.dot(a_hi, w_hi, preferred_element_type=F32)
    acc += jnp.dot(a_hi, w_lo, preferred_element_type=F32)
    acc += jnp.dot(a_lo, w_hi, preferred_element_type=F32)
    o_ref[...] = acc + b_ref[...]


def _ada_call(c, w, b):
    bsz, d = c.shape
    n = w.shape[1]
    tn = 1024
    return pl.pallas_call(
        _ada_kernel,
        grid=(n // tn,),
        in_specs=[pl.BlockSpec((bsz, d), lambda j: (0, 0)),
                  pl.BlockSpec((d, tn), lambda j: (0, j)),
                  pl.BlockSpec((1, tn), lambda j: (0, j))],
        out_specs=pl.BlockSpec((bsz, tn), lambda j: (0, j)),
        out_shape=jax.ShapeDtypeStruct((bsz, n), F32),
        compiler_params=_params(("parallel",)),
        name="adaln",
    )(c, w, b.reshape(1, n))


def _ffn_kernel(x_ref, sh_ref, sc_ref, gt_ref, g_ref, wg_ref, wu_ref, wd_ref, *rest, tf, final, nsplit):
    o_ref = rest[-1]
    rows = x_ref.shape[1] // nsplit
    dff = wg_ref.shape[1]
    for h in range(nsplit):
        seg = slice(h * rows, (h + 1) * rows)
        x = x_ref[0, seg]
        u = _modulated_norm(x, g_ref[...], sh_ref[0], sc_ref[0]).astype(BF16)
        acc = None
        for c in range(dff // tf):
            cols = slice(c * tf, (c + 1) * tf)
            gate = jnp.dot(u, wg_ref[:, cols], preferred_element_type=F32)
            up = jnp.dot(u, wu_ref[:, cols], preferred_element_type=F32)
            act = (gate * jax.nn.sigmoid(gate) * up).astype(BF16)
            part = jnp.dot(act, wd_ref[cols, :], preferred_element_type=F32)
            acc = part if acc is None else acc + part
        y = x + (FFN_RES * gt_ref[0]) * acc
        if final:
            gfin_ref = rest[0]
            ms = jnp.mean(y * y, axis=-1, keepdims=True)
            y = (y * lax.rsqrt(ms + RMS_EPS)) * gfin_ref[...]
        o_ref[0, seg] = y


def _ffn_call(h, shift, scale, gate, g, wg, wu, wd, g_final=None, *, tm=1024, tf=256, nsplit=2):
    bsz, s, d = h.shape
    dff = wg.shape[1]
    final = g_final is not None
    vec = pl.BlockSpec((1, 1, d), lambda b, i: (b, 0, 0))
    in_specs = [pl.BlockSpec((1, tm, d), lambda b, i: (b, i, 0)), vec, vec, vec,
                _const_spec((1, d)), _const_spec((d, dff)), _const_spec((d, dff)), _const_spec((dff, d))]
    args = [h, shift, scale, gate, g.reshape(1, d), wg, wu, wd]
    if final:
        in_specs.append(_const_spec((1, d)))
        args.append(g_final.reshape(1, d))
    return pl.pallas_call(
        functools.partial(_ffn_kernel, tf=tf, final=final, nsplit=nsplit),
        grid=(bsz, s // tm),
        in_specs=in_specs,
        out_specs=pl.BlockSpec((1, tm, d), lambda b, i: (b, i, 0)),
        out_shape=jax.ShapeDtypeStruct((bsz, s, d), F32),
        compiler_params=_params(("parallel", "parallel")),
        name="ffn_final" if final else "ffn",
    )(*args)


N_KPAIR = 3
Q_COLS = N_HEADS * HEAD_DIM
NN_COLS = (N_KPAIR + N_KV) * LANES
N_VSLOT = N_KPAIR * N_KV
GATE_ROWS = 32
VT_ROWS = N_VSLOT * HEAD_DIM
T_ROWS = 2 * Q_COLS + VT_ROWS + GATE_ROWS
KEY_CHUNK = LANES


def _proj_kernel(x_ref, sh_ref, sc_ref, g_ref, wn_ref, wt_ref, kp_ref, kvc_ref, qn_ref, qs_ref, vt_ref, gn_ref,
                 slab_scr, *, nsplit):
    rows = x_ref.shape[1] // nsplit

    def products(h):
        u = _modulated_norm(x_ref[0, h * rows:(h + 1) * rows], g_ref[...], sh_ref[0], sc_ref[0]).astype(BF16)
        return (jnp.dot(u, wn_ref[...], preferred_element_type=F32),
                lax.dot_general(wt_ref[...], u, (((1,), (1,)), ((), ())), preferred_element_type=F32))

    nxt = products(0)
    for h in range(nsplit):
        nn, tt = nxt
        if h + 1 < nsplit:
            nxt = products(h + 1)
        seg = slice(h * rows, (h + 1) * rows)
        for i in range(N_KPAIR):
            kp_ref[0, i, seg] = nn[:, i * LANES:(i + 1) * LANES].astype(BF16)
        crow = slice(h * rows // CMP_STRIDE, (h + 1) * rows // CMP_STRIDE)
        for i in range(N_KV):
            slab_scr[...] = nn[:, (N_KPAIR + i) * LANES:(N_KPAIR + i + 1) * LANES]
            for t in range(CMP_STRIDE):
                part = slab_scr[pl.ds(t, rows // CMP_STRIDE, stride=CMP_STRIDE), :]
                kvc_ref[0, i, crow, t * LANES:(t + 1) * LANES] = part.astype(BF16)
        qn_ref[0, :, seg] = (tt[0:Q_COLS] * (ATTN_SCALE * LOG2E)).astype(BF16)
        qs_ref[0, :, seg] = (tt[Q_COLS:2 * Q_COLS] * (ATTN_SCALE * LOG2E)).astype(BF16)
        base = 2 * Q_COLS
        for s in range(N_VSLOT):
            for c in range(rows // KEY_CHUNK):
                vt_ref[0, s, h * (rows // KEY_CHUNK) + c] = tt[base + s * HEAD_DIM: base + (s + 1) * HEAD_DIM,
                                                               c * KEY_CHUNK:(c + 1) * KEY_CHUNK].astype(BF16)
        gn_ref[0, :, seg] = tt[base + VT_ROWS:]


def _proj_call(h, shift, scale, g, wn, wt, *, tm=1024, nsplit=2):
    bsz, s, d = h.shape
    vec = pl.BlockSpec((1, 1, d), lambda b, i: (b, 0, 0))
    nck = tm // KEY_CHUNK
    return pl.pallas_call(
        functools.partial(_proj_kernel, nsplit=nsplit),
        grid=(bsz, s // tm),
        in_specs=[pl.BlockSpec((1, tm, d), lambda b, i: (b, i, 0)), vec, vec,
                  _const_spec((1, d)), _const_spec((d, NN_COLS)), _const_spec((T_ROWS, d))],
        out_specs=[pl.BlockSpec((1, N_KPAIR, tm, LANES), lambda b, i: (b, 0, i, 0)),
                   pl.BlockSpec((1, N_KV, tm // CMP_STRIDE, CMP_STRIDE * LANES), lambda b, i: (b, 0, i, 0)),
                   pl.BlockSpec((1, Q_COLS, tm), lambda b, i: (b, 0, i)),
                   pl.BlockSpec((1, Q_COLS, tm), lambda b, i: (b, 0, i)),
                   pl.BlockSpec((1, N_VSLOT, nck, HEAD_DIM, KEY_CHUNK), lambda b, i: (b, 0, i, 0, 0)),
                   pl.BlockSpec((1, GATE_ROWS, tm), lambda b, i: (b, 0, i))],
        out_shape=[jax.ShapeDtypeStruct((bsz, N_KPAIR, s, LANES), BF16),
                   jax.ShapeDtypeStruct((bsz, N_KV, s // CMP_STRIDE, CMP_STRIDE * LANES), BF16),
                   jax.ShapeDtypeStruct((bsz, Q_COLS, s), BF16),
                   jax.ShapeDtypeStruct((bsz, Q_COLS, s), BF16),
                   jax.ShapeDtypeStruct((bsz, N_VSLOT, s // KEY_CHUNK, HEAD_DIM, KEY_CHUNK), BF16),
                   jax.ShapeDtypeStruct((bsz, GATE_ROWS, s), F32)],
        scratch_shapes=[pltpu.VMEM((tm // nsplit, LANES), F32)],
        compiler_params=_params(("parallel", "parallel")),
        name="mixer_proj",
    )(h, shift, scale, g.reshape(1, d), wn, wt)


def _cmp_kernel(a_ref, pa_ref, pb_ref, w1a_ref, w1b_ref, w2_ref, o_ref, ot_ref):
    nb, ng, n, _ = a_ref.shape
    bias = (jnp.dot(pa_ref[...], w1a_ref[...], preferred_element_type=F32)
            + jnp.dot(pb_ref[...], w1b_ref[...], preferred_element_type=F32))[0:1]
    a = a_ref[...].reshape(nb * ng * n, a_ref.shape[3])
    first = jnp.dot(a, w1a_ref[...], preferred_element_type=F32)
    second = jnp.dot(a, w1b_ref[...], preferred_element_type=F32)
    for i in range(nb * ng):
        rows = slice(i * n, (i + 1) * n)
        hid = first[rows] + pltpu.roll(second[rows], n - 1, axis=0) + bias
        hid = jax.nn.gelu(hid)
        out = jnp.dot(hid.astype(BF16), w2_ref[...], preferred_element_type=F32)
        o_ref[i // ng, i % ng] = out.astype(BF16)
        ot_ref[i // ng, i % ng] = out.T.astype(BF16)


def _cmp_call(kv_chunks, pa, pb, w1a, w1b, w2):
    bsz, _, nchunk, width = kv_chunks.shape
    nb = 2 if bsz % 2 == 0 else 1
    return pl.pallas_call(
        _cmp_kernel,
        grid=(bsz // nb,),
        in_specs=[pl.BlockSpec((nb, N_KV, nchunk, width), lambda b: (b, 0, 0, 0)),
                  _const_spec(pa.shape), _const_spec(pb.shape),
                  _const_spec(w1a.shape), _const_spec(w1b.shape), _const_spec(w2.shape)],
        out_specs=[pl.BlockSpec((nb, N_KV, nchunk, LANES), lambda b: (b, 0, 0, 0)),
                   pl.BlockSpec((nb, N_KV, LANES, nchunk), lambda b: (b, 0, 0, 0))],
        out_shape=[jax.ShapeDtypeStruct((bsz, N_KV, nchunk, LANES), BF16),
                   jax.ShapeDtypeStruct((bsz, N_KV, LANES, nchunk), BF16)],
        compiler_params=_params(("parallel",)),
        name="nsa_compress",
    )(kv_chunks, pa, pb, w1a, w1b, w2)


POS_HI, POS_LO, POS_ONE = 96, 99, 102
MAX_SEL_BLOCKS = POS_HI


def _position_tables(seq):
    key = np.arange(seq)
    kpos = np.zeros((seq, LANES), np.float32)
    kpos[key, key // SEL_LEN] = 1.0
    kpos[:, POS_HI:POS_HI + 3] = ((key // SEL_LEN) * SEL_LEN)[:, None]
    kpos[:, POS_LO:POS_LO + 3] = (key % SEL_LEN)[:, None]
    kpos[:, POS_ONE:POS_ONE + 3] = 1.0
    ncmp = seq // CMP_STRIDE
    cpos = np.zeros((ncmp, LANES), np.float32)
    cpos[:, POS_HI:POS_HI + 3] = (np.arange(ncmp) * CMP_STRIDE)[:, None]
    cpos[:, POS_LO:POS_LO + 3] = CMP_LEN - 1
    cpos[:, POS_ONE:POS_ONE + 3] = 1.0
    return jnp.asarray(kpos, BF16), jnp.asarray(cpos, BF16)


def _attn_kernel(sinks_ref, qn_ref, qs_ref, gn_ref, kp_ref, vt_ref, kvc_ref, kvct_ref, kpos_ref, cpos_ref,
                 ya_ref, yb_ref, s_scr, m_scr, acc_scr, blocks_ref, *, seq, tq, tqs):
    qi = pl.program_id(1)
    t0 = qi * tq
    cols = N_REP * tq
    nsel = seq // SEL_LEN
    ncmp = kvc_ref.shape[2]
    n_top = min(SEL_TOPN, nsel)

    def stacked_q(q_ref, g, low, width, lane0=0, sel_t=None):
        sub = lax.broadcasted_iota(jnp.int32, (LANES, width), 0)
        zero_q = jnp.zeros((HEAD_DIM, width), BF16)
        blocks = []
        for r in range(N_REP):
            h = g * N_REP + r
            q = q_ref[0, h * HEAD_DIM:(h + 1) * HEAD_DIM, lane0:lane0 + width]
            qpart = jnp.concatenate([q, zero_q] if low else [zero_q, q], axis=0)
            ext = jnp.zeros((LANES, width), F32)
            for i, c in enumerate(SLOPE_TERMS[h]):
                ext = jnp.where((sub == POS_HI + i) | (sub == POS_LO + i), c, ext)
            tpos = (t0 + lane0 + lax.broadcasted_iota(jnp.int32, (1, width), 1)).astype(F32)
            for i, part in enumerate(_split3(-(SLOPES[h] * LOG2E) * tpos)):
                ext = jnp.where(sub == POS_ONE + i, part.astype(F32), ext)
            if sel_t is not None:
                ext = ext + jnp.where(sub < nsel, (sel_t - 1.0) * (-NEG_INF), 0.0)
            blocks.append(jnp.concatenate([qpart, ext.astype(BF16)], axis=0))
        return jnp.concatenate(blocks, axis=1)

    def masked(s, mask, width):
        return jnp.concatenate(
            [jnp.where(mask, s[:, r * width:(r + 1) * width], NEG_INF) for r in range(N_REP)], axis=1)

    def biased(s, bias, width):
        sb = s.astype(BF16)
        return jnp.concatenate([sb[:, r * width:(r + 1) * width] + bias for r in range(N_REP)], axis=1)

    def value_rows(vt):
        return jnp.concatenate([jnp.ones_like(vt), vt], axis=0)

    def load_vt(slot, first_chunk, nchunks):
        return jnp.concatenate([vt_ref[0, slot, first_chunk + j] for j in range(nchunks)], axis=1)

    def band_geometry(window, sub):
        span = tqs + -(-window // KEY_CHUNK) * KEY_CHUNK
        ts = t0 + sub * tqs
        k_start = pl.multiple_of(jnp.maximum(ts + tqs - span, 0), KEY_CHUNK)
        d = ((ts - k_start) + lax.broadcasted_iota(jnp.int32, (span, tqs), 1)
             - lax.broadcasted_iota(jnp.int32, (span, tqs), 0))
        return span, k_start, jnp.where((d >= 0) & (d < window), 0.0, NEG_INF).astype(BF16)

    def band_scores(qt, pair, geometry):
        span, k_start, bias = geometry
        kaug = jnp.concatenate([kp_ref[0, pair, pl.ds(k_start, span), :], kpos_ref[pl.ds(k_start, span), :]], axis=1)
        return biased(jnp.dot(kaug, qt, preferred_element_type=F32), bias, tqs)

    def band_output(s, vslot, geometry, sink_row=None):
        span, k_start, _ = geometry
        m = jnp.max(s, axis=0, keepdims=True)
        if sink_row is not None:
            m = jnp.maximum(m, sink_row)
        p = jnp.exp2(s - m)
        vt = load_vt(vslot, k_start // KEY_CHUNK, span // KEY_CHUNK)
        o = jnp.dot(value_rows(vt), p, preferred_element_type=F32)
        l = o[:HEAD_DIM]
        if sink_row is not None:
            l = l + jnp.exp2(sink_row - m).astype(F32)
        return o[HEAD_DIM:] / l

    tc = t0 + lax.broadcasted_iota(jnp.int32, (ncmp, tq), 1)
    cend = lax.broadcasted_iota(jnp.int32, (ncmp, tq), 0) * CMP_STRIDE + (CMP_LEN - 1)
    cmask = tc >= cend
    row_ok = (t0 + lax.broadcasted_iota(jnp.int32, (1, tq), 1)) >= (CMP_LEN - 1)
    row_ok = jnp.concatenate([row_ok.astype(F32)] * N_REP, axis=1)
    jn = lax.broadcasted_iota(jnp.int32, (nsel, ncmp), 0) * SEL_LEN
    cn = lax.broadcasted_iota(jnp.int32, (nsel, ncmp), 1) * CMP_STRIDE
    ov_t = jnp.where((cn < jn + SEL_LEN) & (cn + CMP_LEN > jn), 1.0, 0.0).astype(BF16)
    jb = lax.broadcasted_iota(jnp.int32, (nsel, tq), 0)
    cur = (t0 + lax.broadcasted_iota(jnp.int32, (nsel, tq), 1)) // SEL_LEN
    forced = (jb == 0) | (jb == cur) | (jb == cur - 1)
    valid = jb <= cur
    cmp_scores = [jnp.dot(jnp.concatenate([kvc_ref[0, g], cpos_ref[...]], axis=1), stacked_q(qn_ref, g, True, tq),
                          preferred_element_type=F32) for g in range(N_KV)]
    o_cmp, scores_sel = [], []
    for g in range(N_KV):
        s = masked(cmp_scores[g], cmask, tq)
        e = jnp.exp2(s - jnp.max(s, axis=0, keepdims=True))
        p = e * (row_ok / jnp.sum(e, axis=0, keepdims=True))
        o_cmp.append(jnp.dot(value_rows(kvct_ref[0, g, HEAD_DIM:, :]), p.astype(BF16),
                             preferred_element_type=F32)[HEAD_DIM:])
        psum = p[:, 0:tq]
        for r in range(1, N_REP):
            psum = psum + p[:, r * tq:(r + 1) * tq]
        imp_t = sum(jnp.dot(ov_t, part, preferred_element_type=F32) for part in _split3(psum))
        scores_sel.append(jnp.where(forced, SEL_BONUS, jnp.where(valid, imp_t, -1.0)))

    items = []
    for sub in range(tq // tqs):
        geo_win = band_geometry(NSA_WINDOW, sub)
        geo_swa = band_geometry(SWA_WINDOW, sub)
        for g in range(N_KV):
            sink_row = jnp.concatenate(
                [jnp.full((1, tqs), sinks_ref[g * N_REP + r] * LOG2E, F32) for r in range(N_REP)],
                axis=1).astype(BF16)
            items.append((qn_ref, g, sub, 1, N_KV + g, geo_win, None))
            items.append((qs_ref, g, sub, 2, 2 * N_KV + g, geo_swa, sink_row))

    def item_scores(item):
        q_ref, g, sub, pair, _, geo, _ = item
        return band_scores(stacked_q(q_ref, g, g == 0, tqs, sub * tqs), pair, geo)

    outs, ahead = [], 2
    pending = [item_scores(items[k]) for k in range(ahead)]
    for k, item in enumerate(items):
        if k + ahead < len(items):
            pending.append(item_scores(items[k + ahead]))
        outs.append(band_output(pending.pop(0), item[4], item[5], item[6]))

    gsig = jax.nn.sigmoid(gn_ref[0])
    partial, yb_cols = [], []
    for sub in range(tq // tqs):
        lane0 = sub * tqs
        gates = gsig[:, lane0:lane0 + tqs]
        rows, yb_rows = [], []
        for g in range(N_KV):
            o_win, o_swa = outs[(sub * N_KV + g) * 2], outs[(sub * N_KV + g) * 2 + 1]
            for r in range(N_REP):
                c = (g * N_REP + r) * 3
                wide = slice(r * tq + lane0, r * tq + lane0 + tqs)
                rows.append(gates[c:c + 1] * o_cmp[g][:, wide] + gates[c + 2:c + 3] * o_win[:, r * tqs:(r + 1) * tqs])
            yb_rows += [o_swa[:, r * tqs:(r + 1) * tqs] for r in range(N_REP)]
        partial.append(rows)
        yb_cols.append(jnp.concatenate(yb_rows, axis=0))
    yb_ref[0] = jnp.concatenate(yb_cols, axis=1).astype(BF16)

    qt_sel, chosen = [], None
    for score in scores_sel:
        rank = [jnp.zeros((8, tq), F32) for _ in range(nsel // 8)]
        for i in range(nsel):
            row = score[i:i + 1, :]
            for k in range(nsel // 8):
                blk = score[8 * k:8 * (k + 1)]
                ge = jnp.where(row >= blk, 1.0, 0.0)
                gt = jnp.where(row > blk, 1.0, 0.0)
                if 8 * k > i:
                    beats = ge
                elif 8 * k + 7 <= i:
                    beats = gt
                else:
                    beats = jnp.where(lax.broadcasted_iota(jnp.int32, (8, tq), 0) > i - 8 * k, ge, gt)
                rank[k] = rank[k] + beats
        sel_t = jnp.where(jnp.concatenate(rank, axis=0) < n_top, 1.0, 0.0)
        sel_t = jnp.concatenate([sel_t, jnp.zeros((LANES - nsel, tq), F32)], axis=0)
        chosen = sel_t if chosen is None else chosen + sel_t
        qt_sel.append(stacked_q(qn_ref, len(qt_sel), len(qt_sel) == 0, tq, 0, sel_t))

    nck = tq // KEY_CHUNK
    per_blk = tq // SEL_LEN
    n_prev = jnp.int32(0)
    for kb in range(seq // tq - 1):
        needed = (jnp.max(chosen[kb * per_blk:(kb + 1) * per_blk, :]) > 0.5) & (kb < qi)
        blocks_ref[n_prev] = kb
        n_prev = n_prev + needed.astype(jnp.int32)
    blocks_ref[n_prev] = qi

    def key_rows(ids):
        starts = [pl.multiple_of(kb * tq, tq) for kb in ids]
        return jnp.concatenate([jnp.concatenate([kp_ref[0, 0, pl.ds(k0, tq), :] for k0 in starts], axis=0),
                                jnp.concatenate([kpos_ref[pl.ds(k0, tq), :] for k0 in starts], axis=0)], axis=1)

    def score_step(ids, slot, bias=None):
        n = len(ids)
        kaug = key_rows(ids)
        raw = [jnp.dot(kaug, qt_sel[g], preferred_element_type=F32) for g in range(N_KV)]
        for g in range(N_KV):
            s = raw[g].astype(BF16) if bias is None else biased(raw[g], bias, tq)
            s_scr[g, pl.ds(slot, n)] = s.reshape(n, tq, cols)
            m_scr[g] = jnp.maximum(m_scr[g], jnp.max(s.reshape(n * tq // 16, 16, cols), axis=0))

    def value_step(ids, slot):
        n = len(ids)
        ps = [jnp.exp2(s_scr[g, pl.ds(slot, n)].reshape(n * tq, cols) - m_scr[g, 0:1, :]) for g in range(N_KV)]
        for g in range(N_KV):
            vt = jnp.concatenate([load_vt(g, kb * nck, nck) for kb in ids], axis=1)
            acc_scr[g] += jnp.dot(value_rows(vt), ps[g], preferred_element_type=F32)

    for g in range(N_KV):
        m_scr[g] = jnp.full((16, cols), NEG_INF, BF16)
        acc_scr[g] = jnp.zeros((LANES, cols), F32)

    def pass1(i, carry):
        score_step((blocks_ref[2 * i], blocks_ref[2 * i + 1]), 2 * i)
        return carry

    lax.fori_loop(0, n_prev // 2, pass1, 0)
    causal = jnp.where(lax.broadcasted_iota(jnp.int32, (tq, tq), 1) >= lax.broadcasted_iota(jnp.int32, (tq, tq), 0),
                       0.0, NEG_INF).astype(BF16)
    odd = (n_prev % 2) == 1

    @pl.when(odd)
    def _():
        score_step((blocks_ref[n_prev - 1], qi), n_prev - 1,
                   jnp.concatenate([jnp.zeros((tq, tq), BF16), causal], axis=0))

    @pl.when(jnp.logical_not(odd))
    def _():
        score_step((qi,), n_prev, causal)

    for g in range(N_KV):
        m_scr[g] = jnp.broadcast_to(jnp.max(m_scr[g], axis=0, keepdims=True), (16, cols))

    def pass2(i, carry):
        value_step((blocks_ref[2 * i], blocks_ref[2 * i + 1]), 2 * i)
        return carry

    lax.fori_loop(0, (n_prev + 1) // 2, pass2, 0)

    @pl.when(jnp.logical_not(odd))
    def _():
        value_step((qi,), n_prev)

    ya_cols = []
    for sub in range(tq // tqs):
        lane0 = sub * tqs
        rows = []
        for g in range(N_KV):
            o_slc = acc_scr[g, HEAD_DIM:, :] / acc_scr[g, :HEAD_DIM, :]
            for r in range(N_REP):
                c = (g * N_REP + r) * 3 + 1
                wide = slice(r * tq + lane0, r * tq + lane0 + tqs)
                rows.append(partial[sub][g * N_REP + r] + gsig[c:c + 1, lane0:lane0 + tqs] * o_slc[:, wide])
        ya_cols.append(jnp.concatenate(rows, axis=0))
    ya_ref[0] = jnp.concatenate(ya_cols, axis=1).astype(BF16)


def _attn_call(sinks, qn_t, qs_t, gn_t, kp, vt, kvc, kvc_t, *, tq=256, tqs=128):
    bsz, _, s = qn_t.shape
    ncmp = kvc.shape[2]
    assert s % tq == 0 and s // SEL_LEN <= MAX_SEL_BLOCKS and s >= tq + NSA_WINDOW
    kpos, cpos = _position_tables(s)
    qspec = pl.BlockSpec((1, Q_COLS, tq), lambda b, i: (b, 0, i))
    yspec = pl.BlockSpec((1, Q_COLS, tq), lambda b, i: (b, 0, i))
    cols = N_REP * tq
    return pl.pallas_call(
        functools.partial(_attn_kernel, seq=s, tq=tq, tqs=tqs),
        grid=(bsz, s // tq),
        in_specs=[pl.BlockSpec(memory_space=pltpu.SMEM),
                  qspec, qspec,
                  pl.BlockSpec((1, GATE_ROWS, tq), lambda b, i: (b, 0, i)),
                  pl.BlockSpec((1, N_KPAIR, s, LANES), lambda b, i: (b, 0, 0, 0)),
                  pl.BlockSpec((1, N_VSLOT, s // KEY_CHUNK, HEAD_DIM, KEY_CHUNK), lambda b, i: (b, 0, 0, 0, 0)),
                  pl.BlockSpec((1, N_KV, ncmp, LANES), lambda b, i: (b, 0, 0, 0)),
                  pl.BlockSpec((1, N_KV, LANES, ncmp), lambda b, i: (b, 0, 0, 0)),
                  _const_spec((s, LANES)), _const_spec((ncmp, LANES))],
        out_specs=[yspec, yspec],
        out_shape=[jax.ShapeDtypeStruct((bsz, Q_COLS, s), BF16),
                   jax.ShapeDtypeStruct((bsz, Q_COLS, s), BF16)],
        scratch_shapes=[pltpu.VMEM((N_KV, s // tq, tq, cols), BF16),
                        pltpu.VMEM((N_KV, 16, cols), BF16),
                        pltpu.VMEM((N_KV, LANES, cols), F32),
                        pltpu.SMEM((s // tq + 1,), jnp.int32)],
        compiler_params=_params(("parallel", "arbitrary")),
        name="hybrid_attention",
    )(sinks, qn_t, qs_t, gn_t, kp, vt, kvc, kvc_t, kpos, cpos)


def _merge_kernel(x_ref, sh_ref, sc_ref, gt_ref, g_ref, ya_ref, yb_ref, wgm_ref, wa_ref, wb_ref, wo_ref, o_ref, *,
                  nsplit):
    d = x_ref.shape[-1]
    rows = x_ref.shape[1] // nsplit

    def products(h):
        seg = slice(h * rows, (h + 1) * rows)
        u = _modulated_norm(x_ref[0, seg], g_ref[...], sh_ref[0], sc_ref[0]).astype(BF16)
        tn = (((0,), (0,)), ((), ()))
        return (lax.dot_general(ya_ref[0, :, seg], wa_ref[...], tn, preferred_element_type=F32),
                jnp.dot(u, wgm_ref[:, :d], preferred_element_type=F32),
                lax.dot_general(yb_ref[0, :, seg], wb_ref[...], tn, preferred_element_type=F32),
                jnp.dot(u, wgm_ref[:, d:], preferred_element_type=F32))

    nxt = products(0)
    for h in range(nsplit):
        up_a, gate_a, up_b, gate_b = nxt
        if h + 1 < nsplit:
            nxt = products(h + 1)
        merged = jax.nn.sigmoid(gate_a) * up_a + jax.nn.sigmoid(gate_b) * up_b
        y = jnp.dot(merged.astype(BF16), wo_ref[...], preferred_element_type=F32)
        seg = slice(h * rows, (h + 1) * rows)
        o_ref[0, seg] = x_ref[0, seg] + gt_ref[0] * y


def _merge_call(h, shift, scale, gate, g, ya, yb, wgm, wa, wb, wo, *, tm=1024, nsplit=4):
    bsz, s, d = h.shape
    vec = pl.BlockSpec((1, 1, d), lambda b, i: (b, 0, 0))
    tok = pl.BlockSpec((1, tm, d), lambda b, i: (b, i, 0))
    ysp = pl.BlockSpec((1, Q_COLS, tm), lambda b, i: (b, 0, i))
    return pl.pallas_call(
        functools.partial(_merge_kernel, nsplit=nsplit),
        grid=(bsz, s // tm),
        in_specs=[tok, vec, vec, vec, _const_spec((1, d)), ysp, ysp,
                  _const_spec(wgm.shape), _const_spec(wa.shape), _const_spec(wb.shape), _const_spec(wo.shape)],
        out_specs=tok,
        out_shape=jax.ShapeDtypeStruct((bsz, s, d), F32),
        compiler_params=_params(("parallel", "parallel")),
        name="mixer_merge",
    )(h, shift, scale, gate, g.reshape(1, d), ya, yb, wgm, wa, wb, wo)


def _proj_column_ranges():
    kvw = 2 * N_KV * HEAD_DIM
    off_qn = 0
    off_c = off_qn + Q_COLS
    off_s = off_c + kvw
    off_w = off_s + kvw
    off_gn = off_w + kvw
    off_qs = off_gn + 3 * N_HEADS
    off_b = off_qs + Q_COLS
    off_gm = off_b + kvw
    half = N_KV * HEAD_DIM
    nn = [(off, off + half) for off in (off_s, off_w, off_b)]
    for g in range(N_KV):
        nn += [(off_c + g * HEAD_DIM, off_c + (g + 1) * HEAD_DIM),
               (off_c + half + g * HEAD_DIM, off_c + half + (g + 1) * HEAD_DIM)]
    tt = [(off_qn, off_qn + Q_COLS), (off_qs, off_qs + Q_COLS)]
    tt += [(off + half, off + 2 * half) for off in (off_s, off_w, off_b)]
    tt += [(off_gn, off_gn + 3 * N_HEADS)]
    return nn, tt, off_gm


def _take_columns(w, ranges, width):
    parts = [w[:, a:b] for a, b in ranges]
    have = sum(b - a for a, b in ranges)
    if width > have:
        parts.append(jnp.zeros((w.shape[0], width - have), w.dtype))
    return jnp.concatenate(parts, axis=1)


def _compress_weights(pos_k, w1_k, w2_k, pos_v, w1_v, w2_v):
    half = CMP_LEN // 2
    zk = jnp.zeros((half, HEAD_DIM, CMP_HIDDEN), F32)

    def w1_half(sl):
        wk = jnp.concatenate([w1_k[sl], zk], axis=-1)
        wv = jnp.concatenate([zk, w1_v[sl]], axis=-1)
        return jnp.concatenate([wk, wv], axis=1).reshape(half * 2 * HEAD_DIM, 2 * CMP_HIDDEN).astype(BF16)

    def pos_half(sl):
        p = jnp.concatenate([pos_k[sl], pos_v[sl]], axis=1).reshape(1, half * 2 * HEAD_DIM)
        return jnp.broadcast_to(p, (8, p.shape[1])).astype(BF16)

    z2 = jnp.zeros((CMP_HIDDEN, HEAD_DIM), F32)
    w2 = jnp.concatenate([jnp.concatenate([w2_k, z2], axis=1),
                          jnp.concatenate([z2, w2_v], axis=1)], axis=0).astype(BF16)
    lo, hi = slice(0, half), slice(half, CMP_LEN)
    return pos_half(lo), pos_half(hi), w1_half(lo), w1_half(hi), w2


def kernel(x, c, w_ada, b_ada, g_ffn1, w1_gate, w1_up, w1_down, g_mix, w_in, cmp_pos_k, cmp_w1_k, cmp_w2_k,
           cmp_pos_v, cmp_w1_v, cmp_w2_v, sinks, w_up_a, w_up_b, w_out, g_ffn2, w2_gate, w2_up, w2_down, g_final):
    bsz, seq, d = x.shape
    depth = w_ada.shape[0]
    nn_ranges, t_ranges, off_gm = _proj_column_ranges()
    h = x
    for l in range(depth):
        mod = _ada_call(c, w_ada[l], b_ada[l])
        sh1, sc1, gt1, sh2, sc2, gt2, sh3, sc3, gt3 = [m.reshape(bsz, 1, d) for m in jnp.split(mod, 9, axis=-1)]
        last = l == depth - 1

        h = _ffn_call(h, sh1, sc1, gt1, g_ffn1[l],
                      w1_gate[l].astype(BF16), w1_up[l].astype(BF16), w1_down[l].astype(BF16))

        w_nn = _take_columns(w_in[l], nn_ranges, NN_COLS).astype(BF16)
        w_t = _take_columns(w_in[l], t_ranges, T_ROWS).T.astype(BF16)
        kp, kvc_in, qn_t, qs_t, vt, gn_t = _proj_call(h, sh2, sc2, g_mix[l], w_nn, w_t)

        pa, pb, w1a, w1b, w2c = _compress_weights(cmp_pos_k[l], cmp_w1_k[l], cmp_w2_k[l],
                                                  cmp_pos_v[l], cmp_w1_v[l], cmp_w2_v[l])
        kvc, kvc_t = _cmp_call(kvc_in, pa, pb, w1a, w1b, w2c)

        ya, yb = _attn_call(sinks[l].reshape(-1), qn_t, qs_t, gn_t, kp, vt, kvc, kvc_t)

        h = _merge_call(h, sh2, sc2, gt2, g_mix[l], ya, yb,
                        w_in[l][:, off_gm:].astype(BF16), w_up_a[l].astype(BF16),
                        w_up_b[l].astype(BF16), w_out[l].astype(BF16))

        h = _ffn_call(h, sh3, sc3, gt3, g_ffn2[l],
                      w2_gate[l].astype(BF16), w2_up[l].astype(BF16), w2_down[l].astype(BF16),
                      g_final if last else None)
    return h
```

```python
import functools

import numpy as np
import jax
import jax.numpy as jnp
from jax import lax
from jax.experimental import pallas as pl
from jax.experimental.pallas import tpu as pltpu

F32 = jnp.float32
BF16 = jnp.bfloat16

HEAD_DIM = 64
N_HEADS = 8
N_KV = 2
N_REP = N_HEADS // N_KV
CMP_LEN = 32
CMP_STRIDE = 16
CMP_HIDDEN = 256
SEL_LEN = 64
SEL_TOPN = 8
NSA_WINDOW = 512
SWA_WINDOW = 128
FFN_RES = 0.5
RMS_EPS = 1e-6
NEG_INF = -1e30
SEL_BONUS = 1e4
ATTN_SCALE = HEAD_DIM ** -0.5

LANES = 128
VMEM_LIMIT = 56 * 1024 * 1024

SLOPES = [2.0 ** (-8.0 * (h + 1) / N_HEADS) for h in range(N_HEADS)]
LOG2E = 1.4426950408889634


def _bf16_terms(x, n=3):
    terms = []
    for _ in range(n):
        t = float(np.asarray(x, np.float32).astype(BF16).astype(np.float32))
        terms.append(t)
        x = x - t
    return terms


SLOPE_TERMS = [_bf16_terms(s * LOG2E) for s in SLOPES]


def _const_spec(shape):
    n = len(shape)
    return pl.BlockSpec(shape, lambda *_: (0,) * n, pipeline_mode=pl.Buffered(1))


def _params(sem):
    return pltpu.CompilerParams(dimension_semantics=sem, vmem_limit_bytes=VMEM_LIMIT)


def _modulated_norm(x, g, shift, scale):
    ms = jnp.mean(x * x, axis=-1, keepdims=True)
    y = x * lax.rsqrt(ms + RMS_EPS)
    return (y * g) * (1.0 + scale) + shift


def _split3(a):
    hi = a.astype(BF16)
    r1 = a - hi.astype(F32)
    mid = r1.astype(BF16)
    lo = (r1 - mid.astype(F32)).astype(BF16)
    return hi, mid, lo


def _ada_kernel(c_ref, w_ref, b_ref, o_ref):
    c = c_ref[...]
    a = c * jax.nn.sigmoid(c)
    a_hi = a.astype(BF16)
    a_lo = (a - a_hi.astype(F32)).astype(BF16)
    w = w_ref[...]
    w_hi = w.astype(BF16)
    w_lo = (w - w_hi.astype(F32)).astype(BF16)
    acc = jnp.dot(a_hi, w_hi, preferred_element_type=F32)
    acc += jnp.dot(a_hi, w_lo, preferred_element_type=F32)
    acc += jnp.dot(a_lo, w_hi, preferred_element_type=F32)
    o_ref[...] = acc + b_ref[...]


def _ada_call(c, w, b):
    bsz, d = c.shape
    n = w.shape[1]
    tn = 1024
    return pl.pallas_call(
        _ada_kernel,
        grid=(n // tn,),
        in_specs=[pl.BlockSpec((bsz, d), lambda j: (0, 0)),
                  pl.BlockSpec((d, tn), lambda j: (0, j)),
                  pl.BlockSpec((1, tn), lambda j: (0, j))],
        out_specs=pl.BlockSpec((bsz, tn), lambda j: (0, j)),
        out_shape=jax.ShapeDtypeStruct((bsz, n), F32),
        compiler_params=_params(("parallel",)),
        name="adaln",
    )(c, w, b.reshape(1, n))


def _ffn_kernel(x_ref, sh_ref, sc_ref, gt_ref, g_ref, wg_ref, wu_ref, wd_ref, *rest, tf, final, nsplit):
    o_ref = rest[-1]
    rows = x_ref.shape[1] // nsplit
    dff = wg_ref.shape[1]
    for h in range(nsplit):
        seg = slice(h * rows, (h + 1) * rows)
        x = x_ref[0, seg]
        u = _modulated_norm(x, g_ref[...], sh_ref[0], sc_ref[0]).astype(BF16)
        acc = None
        for c in range(dff // tf):
            cols = slice(c * tf, (c + 1) * tf)
            gate = jnp.dot(u, wg_ref[:, cols].astype(BF16), preferred_element_type=F32)
            up = jnp.dot(u, wu_ref[:, cols].astype(BF16), preferred_element_type=F32)
            act = (gate * jax.nn.sigmoid(gate) * up).astype(BF16)
            part = jnp.dot(act, wd_ref[cols, :].astype(BF16), preferred_element_type=F32)
            acc = part if acc is None else acc + part
        y = x + (FFN_RES * gt_ref[0]) * acc
        if final:
            gfin_ref = rest[0]
            ms = jnp.mean(y * y, axis=-1, keepdims=True)
            y = (y * lax.rsqrt(ms + RMS_EPS)) * gfin_ref[...]
        o_ref[0, seg] = y


def _ffn_call(h, shift, scale, gate, g, wg, wu, wd, g_final=None, *, tm=1024, tf=256, nsplit=2):
    bsz, s, d = h.shape
    dff = wg.shape[1]
    final = g_final is not None
    vec = pl.BlockSpec((1, 1, d), lambda b, i: (b, 0, 0))
    in_specs = [pl.BlockSpec((1, tm, d), lambda b, i: (b, i, 0)), vec, vec, vec,
                _const_spec((1, d)), _const_spec((d, dff)), _const_spec((d, dff)), _const_spec((dff, d))]
    args = [h, shift, scale, gate, g.reshape(1, d), wg, wu, wd]
    if final:
        in_specs.append(_const_spec((1, d)))
        args.append(g_final.reshape(1, d))
    return pl.pallas_call(
        functools.partial(_ffn_kernel, tf=tf, final=final, nsplit=nsplit),
        grid=(bsz, s // tm),
        in_specs=in_specs,
        out_specs=pl.BlockSpec((1, tm, d), lambda b, i: (b, i, 0)),
        out_shape=jax.ShapeDtypeStruct((bsz, s, d), F32),
        compiler_params=_params(("parallel", "parallel")),
        name="ffn_final" if final else "ffn",
    )(*args)


N_KPAIR = 3
Q_COLS = N_HEADS * HEAD_DIM
NN_COLS = (N_KPAIR + N_KV) * LANES
N_VSLOT = N_KPAIR * N_KV
GATE_ROWS = 32
VT_ROWS = N_VSLOT * HEAD_DIM
T_ROWS = 2 * Q_COLS + VT_ROWS + GATE_ROWS
KEY_CHUNK = LANES


def _proj_kernel(x_ref, sh_ref, sc_ref, g_ref, wn_ref, wt_ref, kp_ref, kvc_ref, qn_ref, qs_ref, vt_ref, gn_ref,
                 slab_scr, *, nsplit):
    rows = x_ref.shape[1] // nsplit

    def products(h):
        u = _modulated_norm(x_ref[0, h * rows:(h + 1) * rows], g_ref[...], sh_ref[0], sc_ref[0]).astype(BF16)
        return (jnp.dot(u, wn_ref[...], preferred_element_type=F32),
                lax.dot_general(wt_ref[...], u, (((1,), (1,)), ((), ())), preferred_element_type=F32))

    nxt = products(0)
    for h in range(nsplit):
        nn, tt = nxt
        if h + 1 < nsplit:
            nxt = products(h + 1)
        seg = slice(h * rows, (h + 1) * rows)
        for i in range(N_KPAIR):
            kp_ref[0, i, seg] = nn[:, i * LANES:(i + 1) * LANES].astype(BF16)
        crow = slice(h * rows // CMP_STRIDE, (h + 1) * rows // CMP_STRIDE)
        for i in range(N_KV):
            slab_scr[...] = nn[:, (N_KPAIR + i) * LANES:(N_KPAIR + i + 1) * LANES]
            for t in range(CMP_STRIDE):
                part = slab_scr[pl.ds(t, rows // CMP_STRIDE, stride=CMP_STRIDE), :]
                kvc_ref[0, i, crow, t * LANES:(t + 1) * LANES] = part.astype(BF16)
        qn_ref[0, :, seg] = (tt[0:Q_COLS] * (ATTN_SCALE * LOG2E)).astype(BF16)
        qs_ref[0, :, seg] = (tt[Q_COLS:2 * Q_COLS] * (ATTN_SCALE * LOG2E)).astype(BF16)
        base = 2 * Q_COLS
        for s in range(N_VSLOT):
            for c in range(rows // KEY_CHUNK):
                vt_ref[0, s, h * (rows // KEY_CHUNK) + c] = tt[base + s * HEAD_DIM: base + (s + 1) * HEAD_DIM,
                                                               c * KEY_CHUNK:(c + 1) * KEY_CHUNK].astype(BF16)
        gn_ref[0, :, seg] = tt[base + VT_ROWS:]


def _proj_call(h, shift, scale, g, wn, wt, *, tm=1024, nsplit=2):
    bsz, s, d = h.shape
    vec = pl.BlockSpec((1, 1, d), lambda b, i: (b, 0, 0))
    nck = tm // KEY_CHUNK
    return pl.pallas_call(
        functools.partial(_proj_kernel, nsplit=nsplit),
        grid=(bsz, s // tm),
        in_specs=[pl.BlockSpec((1, tm, d), lambda b, i: (b, i, 0)), vec, vec,
                  _const_spec((1, d)), _const_spec((d, NN_COLS)), _const_spec((T_ROWS, d))],
        out_specs=[pl.BlockSpec((1, N_KPAIR, tm, LANES), lambda b, i: (b, 0, i, 0)),
                   pl.BlockSpec((1, N_KV, tm // CMP_STRIDE, CMP_STRIDE * LANES), lambda b, i: (b, 0, i, 0)),
                   pl.BlockSpec((1, Q_COLS, tm), lambda b, i: (b, 0, i)),
                   pl.BlockSpec((1, Q_COLS, tm), lambda b, i: (b, 0, i)),
                   pl.BlockSpec((1, N_VSLOT, nck, HEAD_DIM, KEY_CHUNK), lambda b, i: (b, 0, i, 0, 0)),
                   pl.BlockSpec((1, GATE_ROWS, tm), lambda b, i: (b, 0, i))],
        out_shape=[jax.ShapeDtypeStruct((bsz, N_KPAIR, s, LANES), BF16),
                   jax.ShapeDtypeStruct((bsz, N_KV, s // CMP_STRIDE, CMP_STRIDE * LANES), BF16),
                   jax.ShapeDtypeStruct((bsz, Q_COLS, s), BF16),
                   jax.ShapeDtypeStruct((bsz, Q_COLS, s), BF16),
                   jax.ShapeDtypeStruct((bsz, N_VSLOT, s // KEY_CHUNK, HEAD_DIM, KEY_CHUNK), BF16),
                   jax.ShapeDtypeStruct((bsz, GATE_ROWS, s), F32)],
        scratch_shapes=[pltpu.VMEM((tm // nsplit, LANES), F32)],
        compiler_params=_params(("parallel", "parallel")),
        name="mixer_proj",
    )(h, shift, scale, g.reshape(1, d), wn, wt)


def _cmp_kernel(a_ref, pa_ref, pb_ref, w1a_ref, w1b_ref, w2_ref, o_ref, ot_ref):
    nb, ng, n, _ = a_ref.shape
    bias = (jnp.dot(pa_ref[...], w1a_ref[...], preferred_element_type=F32)
            + jnp.dot(pb_ref[...], w1b_ref[...], preferred_element_type=F32))[0:1]
    a = a_ref[...].reshape(nb * ng * n, a_ref.shape[3])
    first = jnp.dot(a, w1a_ref[...], preferred_element_type=F32)
    second = jnp.dot(a, w1b_ref[...], preferred_element_type=F32)
    for i in range(nb * ng):
        rows = slice(i * n, (i + 1) * n)
        hid = first[rows] + pltpu.roll(second[rows], n - 1, axis=0) + bias
        hid = jax.nn.gelu(hid)
        out = jnp.dot(hid.astype(BF16), w2_ref[...], preferred_element_type=F32)
        o_ref[i // ng, i % ng] = out.astype(BF16)
        ot_ref[i // ng, i % ng] = out.T.astype(BF16)


def _cmp_call(kv_chunks, pa, pb, w1a, w1b, w2):
    bsz, _, nchunk, width = kv_chunks.shape
    nb = 2 if bsz % 2 == 0 else 1
    return pl.pallas_call(
        _cmp_kernel,
        grid=(bsz // nb,),
        in_specs=[pl.BlockSpec((nb, N_KV, nchunk, width), lambda b: (b, 0, 0, 0)),
                  _const_spec(pa.shape), _const_spec(pb.shape),
                  _const_spec(w1a.shape), _const_spec(w1b.shape), _const_spec(w2.shape)],
        out_specs=[pl.BlockSpec((nb, N_KV, nchunk, LANES), lambda b: (b, 0, 0, 0)),
                   pl.BlockSpec((nb, N_KV, LANES, nchunk), lambda b: (b, 0, 0, 0))],
        out_shape=[jax.ShapeDtypeStruct((bsz, N_KV, nchunk, LANES), BF16),
                   jax.ShapeDtypeStruct((bsz, N_KV, LANES, nchunk), BF16)],
        compiler_params=_params(("parallel",)),
        name="nsa_compress",
    )(kv_chunks, pa, pb, w1a, w1b, w2)


POS_HI, POS_LO, POS_ONE = 96, 99, 102
MAX_SEL_BLOCKS = POS_HI


def _position_tables(seq):
    key = np.arange(seq)
    kpos = np.zeros((seq, LANES), np.float32)
    kpos[key, key // SEL_LEN] = 1.0
    kpos[:, POS_HI:POS_HI + 3] = ((key // SEL_LEN) * SEL_LEN)[:, None]
    kpos[:, POS_LO:POS_LO + 3] = (key % SEL_LEN)[:, None]
    kpos[:, POS_ONE:POS_ONE + 3] = 1.0
    ncmp = seq // CMP_STRIDE
    cpos = np.zeros((ncmp, LANES), np.float32)
    cpos[:, POS_HI:POS_HI + 3] = (np.arange(ncmp) * CMP_STRIDE)[:, None]
    cpos[:, POS_LO:POS_LO + 3] = CMP_LEN - 1
    cpos[:, POS_ONE:POS_ONE + 3] = 1.0
    return jnp.asarray(kpos, BF16), jnp.asarray(cpos, BF16)


def _attn_kernel(sinks_ref, qn_ref, qs_ref, gn_ref, kp_ref, vt_ref, kvc_ref, kvct_ref, kpos_ref, cpos_ref,
                 ya_ref, yb_ref, s_scr, m_scr, acc_scr, blocks_ref, *, seq, tq, tqs):
    qi = pl.program_id(1)
    t0 = qi * tq
    cols = N_REP * tq
    nsel = seq // SEL_LEN
    ncmp = kvc_ref.shape[2]
    n_top = min(SEL_TOPN, nsel)

    def stacked_q(q_ref, g, low, width, lane0=0, sel_t=None):
        sub = lax.broadcasted_iota(jnp.int32, (LANES, width), 0)
        zero_q = jnp.zeros((HEAD_DIM, width), BF16)
        blocks = []
        for r in range(N_REP):
            h = g * N_REP + r
            q = q_ref[0, h * HEAD_DIM:(h + 1) * HEAD_DIM, lane0:lane0 + width]
            qpart = jnp.concatenate([q, zero_q] if low else [zero_q, q], axis=0)
            ext = jnp.zeros((LANES, width), F32)
            for i, c in enumerate(SLOPE_TERMS[h]):
                ext = jnp.where((sub == POS_HI + i) | (sub == POS_LO + i), c, ext)
            tpos = (t0 + lane0 + lax.broadcasted_iota(jnp.int32, (1, width), 1)).astype(F32)
            for i, part in enumerate(_split3(-(SLOPES[h] * LOG2E) * tpos)):
                ext = jnp.where(sub == POS_ONE + i, part.astype(F32), ext)
            if sel_t is not None:
                ext = ext + jnp.where(sub < nsel, (sel_t - 1.0) * (-NEG_INF), 0.0)
            blocks.append(jnp.concatenate([qpart, ext.astype(BF16)], axis=0))
        return jnp.concatenate(blocks, axis=1)

    def masked(s, mask, width):
        return jnp.concatenate(
            [jnp.where(mask, s[:, r * width:(r + 1) * width], NEG_INF) for r in range(N_REP)], axis=1)

    def biased(s, bias, width):
        sb = s.astype(BF16)
        return jnp.concatenate([sb[:, r * width:(r + 1) * width] + bias for r in range(N_REP)], axis=1)

    def value_rows(vt):
        return jnp.concatenate([jnp.ones_like(vt), vt], axis=0)

    def load_vt(slot, first_chunk, nchunks):
        return jnp.concatenate([vt_ref[0, slot, first_chunk + j] for j in range(nchunks)], axis=1)

    def band_geometry(window, sub):
        span = tqs + -(-window // KEY_CHUNK) * KEY_CHUNK
        ts = t0 + sub * tqs
        k_start = pl.multiple_of(jnp.maximum(ts + tqs - span, 0), KEY_CHUNK)
        d = ((ts - k_start) + lax.broadcasted_iota(jnp.int32, (span, tqs), 1)
             - lax.broadcasted_iota(jnp.int32, (span, tqs), 0))
        return span, k_start, jnp.where((d >= 0) & (d < window), 0.0, NEG_INF).astype(BF16)

    def band_scores(qt, pair, geometry):
        span, k_start, bias = geometry
        kaug = jnp.concatenate([kp_ref[0, pair, pl.ds(k_start, span), :], kpos_ref[pl.ds(k_start, span), :]], axis=1)
        return biased(jnp.dot(kaug, qt, preferred_element_type=F32), bias, tqs)

    def band_output(s, vslot, geometry, sink_row=None):
        span, k_start, _ = geometry
        m = jnp.max(s, axis=0, keepdims=True)
        if sink_row is not None:
            m = jnp.maximum(m, sink_row)
        p = jnp.exp2(s - m)
        vt = load_vt(vslot, k_start // KEY_CHUNK, span // KEY_CHUNK)
        o = jnp.dot(value_rows(vt), p, preferred_element_type=F32)
        l = o[:HEAD_DIM]
        if sink_row is not None:
            l = l + jnp.exp2(sink_row - m).astype(F32)
        return o[HEAD_DIM:] / l

    tc = t0 + lax.broadcasted_iota(jnp.int32, (ncmp, tq), 1)
    cend = lax.broadcasted_iota(jnp.int32, (ncmp, tq), 0) * CMP_STRIDE + (CMP_LEN - 1)
    cmask = tc >= cend
    row_ok = (t0 + lax.broadcasted_iota(jnp.int32, (1, tq), 1)) >= (CMP_LEN - 1)
    row_ok = jnp.concatenate([row_ok.astype(F32)] * N_REP, axis=1)
    jn = lax.broadcasted_iota(jnp.int32, (nsel, ncmp), 0) * SEL_LEN
    cn = lax.broadcasted_iota(jnp.int32, (nsel, ncmp), 1) * CMP_STRIDE
    ov_t = jnp.where((cn < jn + SEL_LEN) & (cn + CMP_LEN > jn), 1.0, 0.0).astype(BF16)
    jb = lax.broadcasted_iota(jnp.int32, (nsel, tq), 0)
    cur = (t0 + lax.broadcasted_iota(jnp.int32, (nsel, tq), 1)) // SEL_LEN
    forced = (jb == 0) | (jb == cur) | (jb == cur - 1)
    valid = jb <= cur
    cmp_scores = [jnp.dot(jnp.concatenate([kvc_ref[0, g], cpos_ref[...]], axis=1), stacked_q(qn_ref, g, True, tq),
                          preferred_element_type=F32) for g in range(N_KV)]
    o_cmp, scores_sel = [], []
    for g in range(N_KV):
        s = masked(cmp_scores[g], cmask, tq)
        e = jnp.exp2(s - jnp.max(s, axis=0, keepdims=True))
        p = e * (row_ok / jnp.sum(e, axis=0, keepdims=True))
        o_cmp.append(jnp.dot(value_rows(kvct_ref[0, g, HEAD_DIM:, :]), p.astype(BF16),
                             preferred_element_type=F32)[HEAD_DIM:])
        psum = p[:, 0:tq]
        for r in range(1, N_REP):
            psum = psum + p[:, r * tq:(r + 1) * tq]
        imp_t = sum(jnp.dot(ov_t, part, preferred_element_type=F32) for part in _split3(psum))
        scores_sel.append(jnp.where(forced, SEL_BONUS, jnp.where(valid, imp_t, -1.0)))

    items = []
    for sub in range(tq // tqs):
        geo_win = band_geometry(NSA_WINDOW, sub)
        geo_swa = band_geometry(SWA_WINDOW, sub)
        for g in range(N_KV):
            sink_row = jnp.concatenate(
                [jnp.full((1, tqs), sinks_ref[g * N_REP + r] * LOG2E, F32) for r in range(N_REP)],
                axis=1).astype(BF16)
            items.append((qn_ref, g, sub, 1, N_KV + g, geo_win, None))
            items.append((qs_ref, g, sub, 2, 2 * N_KV + g, geo_swa, sink_row))

    def item_scores(item):
        q_ref, g, sub, pair, _, geo, _ = item
        return band_scores(stacked_q(q_ref, g, g == 0, tqs, sub * tqs), pair, geo)

    outs, ahead = [], 2
    pending = [item_scores(items[k]) for k in range(ahead)]
    for k, item in enumerate(items):
        if k + ahead < len(items):
            pending.append(item_scores(items[k + ahead]))
        outs.append(band_output(pending.pop(0), item[4], item[5], item[6]))

    gsig = jax.nn.sigmoid(gn_ref[0])
    partial, yb_cols = [], []
    for sub in range(tq // tqs):
        lane0 = sub * tqs
        gates = gsig[:, lane0:lane0 + tqs]
        rows, yb_rows = [], []
        for g in range(N_KV):
            o_win, o_swa = outs[(sub * N_KV + g) * 2], outs[(sub * N_KV + g) * 2 + 1]
            for r in range(N_REP):
                c = (g * N_REP + r) * 3
                wide = slice(r * tq + lane0, r * tq + lane0 + tqs)
                rows.append(gates[c:c + 1] * o_cmp[g][:, wide] + gates[c + 2:c + 3] * o_win[:, r * tqs:(r + 1) * tqs])
            yb_rows += [o_swa[:, r * tqs:(r + 1) * tqs] for r in range(N_REP)]
        partial.append(rows)
        yb_cols.append(jnp.concatenate(yb_rows, axis=0))
    yb_ref[0] = jnp.concatenate(yb_cols, axis=1).astype(BF16)

    qt_sel, chosen = [], None
    for score in scores_sel:
        rank = [jnp.zeros((8, tq), F32) for _ in range(nsel // 8)]
        for i in range(nsel):
            row = score[i:i + 1, :]
            for k in range(nsel // 8):
                blk = score[8 * k:8 * (k + 1)]
                ge = jnp.where(row >= blk, 1.0, 0.0)
                gt = jnp.where(row > blk, 1.0, 0.0)
                if 8 * k > i:
                    beats = ge
                elif 8 * k + 7 <= i:
                    beats = gt
                else:
                    beats = jnp.where(lax.broadcasted_iota(jnp.int32, (8, tq), 0) > i - 8 * k, ge, gt)
                rank[k] = rank[k] + beats
        sel_t = jnp.where(jnp.concatenate(rank, axis=0) < n_top, 1.0, 0.0)
        sel_t = jnp.concatenate([sel_t, jnp.zeros((LANES - nsel, tq), F32)], axis=0)
        chosen = sel_t if chosen is None else chosen + sel_t
        qt_sel.append(stacked_q(qn_ref, len(qt_sel), len(qt_sel) == 0, tq, 0, sel_t))

    nck = tq // KEY_CHUNK
    per_blk = tq // SEL_LEN
    n_prev = jnp.int32(0)
    for kb in range(seq // tq - 1):
        needed = (jnp.max(chosen[kb * per_blk:(kb + 1) * per_blk, :]) > 0.5) & (kb < qi)
        blocks_ref[n_prev] = kb
        n_prev = n_prev + needed.astype(jnp.int32)
    blocks_ref[n_prev] = qi

    def key_rows(ids):
        starts = [pl.multiple_of(kb * tq, tq) for kb in ids]
        return jnp.concatenate([jnp.concatenate([kp_ref[0, 0, pl.ds(k0, tq), :] for k0 in starts], axis=0),
                                jnp.concatenate([kpos_ref[pl.ds(k0, tq), :] for k0 in starts], axis=0)], axis=1)

    def score_step(ids, slot, bias=None):
        n = len(ids)
        kaug = key_rows(ids)
        raw = [jnp.dot(kaug, qt_sel[g], preferred_element_type=F32) for g in range(N_KV)]
        for g in range(N_KV):
            s = raw[g].astype(BF16) if bias is None else biased(raw[g], bias, tq)
            s_scr[g, pl.ds(slot, n)] = s.reshape(n, tq, cols)
            m_scr[g] = jnp.maximum(m_scr[g], jnp.max(s.reshape(n * tq // 16, 16, cols), axis=0))

    def value_step(ids, slot):
        n = len(ids)
        ps = [jnp.exp2(s_scr[g, pl.ds(slot, n)].reshape(n * tq, cols) - m_scr[g, 0:1, :]) for g in range(N_KV)]
        for g in range(N_KV):
            vt = jnp.concatenate([load_vt(g, kb * nck, nck) for kb in ids], axis=1)
            acc_scr[g] += jnp.dot(value_rows(vt), ps[g], preferred_element_type=F32)

    for g in range(N_KV):
        m_scr[g] = jnp.full((16, cols), NEG_INF, BF16)
        acc_scr[g] = jnp.zeros((LANES, cols), F32)

    def pass1(i, carry):
        score_step((blocks_ref[2 * i], blocks_ref[2 * i + 1]), 2 * i)
        return carry

    lax.fori_loop(0, n_prev // 2, pass1, 0)
    causal = jnp.where(lax.broadcasted_iota(jnp.int32, (tq, tq), 1) >= lax.broadcasted_iota(jnp.int32, (tq, tq), 0),
                       0.0, NEG_INF).astype(BF16)
    odd = (n_prev % 2) == 1

    @pl.when(odd)
    def _():
        score_step((blocks_ref[n_prev - 1], qi), n_prev - 1,
                   jnp.concatenate([jnp.zeros((tq, tq), BF16), causal], axis=0))

    @pl.when(jnp.logical_not(odd))
    def _():
        score_step((qi,), n_prev, causal)

    for g in range(N_KV):
        m_scr[g] = jnp.broadcast_to(jnp.max(m_scr[g], axis=0, keepdims=True), (16, cols))

    def pass2(i, carry):
        value_step((blocks_ref[2 * i], blocks_ref[2 * i + 1]), 2 * i)
        return carry

    lax.fori_loop(0, (n_prev + 1) // 2, pass2, 0)

    @pl.when(jnp.logical_not(odd))
    def _():
        value_step((qi,), n_prev)

    ya_cols = []
    for sub in range(tq // tqs):
        lane0 = sub * tqs
        rows = []
        for g in range(N_KV):
            o_slc = acc_scr[g, HEAD_DIM:, :] / acc_scr[g, :HEAD_DIM, :]
            for r in range(N_REP):
                c = (g * N_REP + r) * 3 + 1
                wide = slice(r * tq + lane0, r * tq + lane0 + tqs)
                rows.append(partial[sub][g * N_REP + r] + gsig[c:c + 1, lane0:lane0 + tqs] * o_slc[:, wide])
        ya_cols.append(jnp.concatenate(rows, axis=0))
    ya_ref[0] = jnp.concatenate(ya_cols, axis=1).astype(BF16)


def _attn_call(sinks, qn_t, qs_t, gn_t, kp, vt, kvc, kvc_t, *, tq=256, tqs=128):
    bsz, _, s = qn_t.shape
    ncmp = kvc.shape[2]
    assert s % tq == 0 and s // SEL_LEN <= MAX_SEL_BLOCKS and s >= tq + NSA_WINDOW
    kpos, cpos = _position_tables(s)
    qspec = pl.BlockSpec((1, Q_COLS, tq), lambda b, i: (b, 0, i))
    yspec = pl.BlockSpec((1, Q_COLS, tq), lambda b, i: (b, 0, i))
    cols = N_REP * tq
    return pl.pallas_call(
        functools.partial(_attn_kernel, seq=s, tq=tq, tqs=tqs),
        grid=(bsz, s // tq),
        in_specs=[pl.BlockSpec(memory_space=pltpu.SMEM),
                  qspec, qspec,
                  pl.BlockSpec((1, GATE_ROWS, tq), lambda b, i: (b, 0, i)),
                  pl.BlockSpec((1, N_KPAIR, s, LANES), lambda b, i: (b, 0, 0, 0)),
                  pl.BlockSpec((1, N_VSLOT, s // KEY_CHUNK, HEAD_DIM, KEY_CHUNK), lambda b, i: (b, 0, 0, 0, 0)),
                  pl.BlockSpec((1, N_KV, ncmp, LANES), lambda b, i: (b, 0, 0, 0)),
                  pl.BlockSpec((1, N_KV, LANES, ncmp), lambda b, i: (b, 0, 0, 0)),
                  _const_spec((s, LANES)), _const_spec((ncmp, LANES))],
        out_specs=[yspec, yspec],
        out_shape=[jax.ShapeDtypeStruct((bsz, Q_COLS, s), BF16),
                   jax.ShapeDtypeStruct((bsz, Q_COLS, s), BF16)],
        scratch_shapes=[pltpu.VMEM((N_KV, s // tq, tq, cols), BF16),
                        pltpu.VMEM((N_KV, 16, cols), BF16),
                        pltpu.VMEM((N_KV, LANES, cols), F32),
                        pltpu.SMEM((s // tq + 1,), jnp.int32)],
        compiler_params=_params(("parallel", "arbitrary")),
        name="hybrid_attention",
    )(sinks, qn_t, qs_t, gn_t, kp, vt, kvc, kvc_t, kpos, cpos)


def _merge_kernel(x_ref, sh_ref, sc_ref, gt_ref, g_ref, ya_ref, yb_ref, wgm_ref, wa_ref, wb_ref, wo_ref, o_ref, *,
                  nsplit):
    d = x_ref.shape[-1]
    rows = x_ref.shape[1] // nsplit

    def products(h):
        seg = slice(h * rows, (h + 1) * rows)
        u = _modulated_norm(x_ref[0, seg], g_ref[...], sh_ref[0], sc_ref[0]).astype(BF16)
        tn = (((0,), (0,)), ((), ()))
        return (lax.dot_general(ya_ref[0, :, seg], wa_ref[...], tn, preferred_element_type=F32),
                jnp.dot(u, wgm_ref[:, :d], preferred_element_type=F32),
                lax.dot_general(yb_ref[0, :, seg], wb_ref[...], tn, preferred_element_type=F32),
                jnp.dot(u, wgm_ref[:, d:], preferred_element_type=F32))

    nxt = products(0)
    for h in range(nsplit):
        up_a, gate_a, up_b, gate_b = nxt
        if h + 1 < nsplit:
            nxt = products(h + 1)
        merged = jax.nn.sigmoid(gate_a) * up_a + jax.nn.sigmoid(gate_b) * up_b
        y = jnp.dot(merged.astype(BF16), wo_ref[...], preferred_element_type=F32)
        seg = slice(h * rows, (h + 1) * rows)
        o_ref[0, seg] = x_ref[0, seg] + gt_ref[0] * y


def _merge_call(h, shift, scale, gate, g, ya, yb, wgm, wa, wb, wo, *, tm=1024, nsplit=4):
    bsz, s, d = h.shape
    vec = pl.BlockSpec((1, 1, d), lambda b, i: (b, 0, 0))
    tok = pl.BlockSpec((1, tm, d), lambda b, i: (b, i, 0))
    ysp = pl.BlockSpec((1, Q_COLS, tm), lambda b, i: (b, 0, i))
    return pl.pallas_call(
        functools.partial(_merge_kernel, nsplit=nsplit),
        grid=(bsz, s // tm),
        in_specs=[tok, vec, vec, vec, _const_spec((1, d)), ysp, ysp,
                  _const_spec(wgm.shape), _const_spec(wa.shape), _const_spec(wb.shape), _const_spec(wo.shape)],
        out_specs=tok,
        out_shape=jax.ShapeDtypeStruct((bsz, s, d), F32),
        compiler_params=_params(("parallel", "parallel")),
        name="mixer_merge",
    )(h, shift, scale, gate, g.reshape(1, d), ya, yb, wgm, wa, wb, wo)


def _proj_column_ranges():
    kvw = 2 * N_KV * HEAD_DIM
    off_qn = 0
    off_c = off_qn + Q_COLS
    off_s = off_c + kvw
    off_w = off_s + kvw
    off_gn = off_w + kvw
    off_qs = off_gn + 3 * N_HEADS
    off_b = off_qs + Q_COLS
    off_gm = off_b + kvw
    half = N_KV * HEAD_DIM
    nn = [(off, off + half) for off in (off_s, off_w, off_b)]
    for g in range(N_KV):
        nn += [(off_c + g * HEAD_DIM, off_c + (g + 1) * HEAD_DIM),
               (off_c + half + g * HEAD_DIM, off_c + half + (g + 1) * HEAD_DIM)]
    tt = [(off_qn, off_qn + Q_COLS), (off_qs, off_qs + Q_COLS)]
    tt += [(off + half, off + 2 * half) for off in (off_s, off_w, off_b)]
    tt += [(off_gn, off_gn + 3 * N_HEADS)]
    return nn, tt, off_gm


def _take_columns(w, ranges, width):
    parts = [w[:, a:b] for a, b in ranges]
    have = sum(b - a for a, b in ranges)
    if width > have:
        parts.append(jnp.zeros((w.shape[0], width - have), w.dtype))
    return jnp.concatenate(parts, axis=1)


def _compress_weights(pos_k, w1_k, w2_k, pos_v, w1_v, w2_v):
    half = CMP_LEN // 2
    zk = jnp.zeros((half, HEAD_DIM, CMP_HIDDEN), F32)

    def w1_half(sl):
        wk = jnp.concatenate([w1_k[sl], zk], axis=-1)
        wv = jnp.concatenate([zk, w1_v[sl]], axis=-1)
        return jnp.concatenate([wk, wv], axis=1).reshape(half * 2 * HEAD_DIM, 2 * CMP_HIDDEN).astype(BF16)

    def pos_half(sl):
        p = jnp.concatenate([pos_k[sl], pos_v[sl]], axis=1).reshape(1, half * 2 * HEAD_DIM)
        return jnp.broadcast_to(p, (8, p.shape[1])).astype(BF16)

    z2 = jnp.zeros((CMP_HIDDEN, HEAD_DIM), F32)
    w2 = jnp.concatenate([jnp.concatenate([w2_k, z2], axis=1),
                          jnp.concatenate([z2, w2_v], axis=1)], axis=0).astype(BF16)
    lo, hi = slice(0, half), slice(half, CMP_LEN)
    return pos_half(lo), pos_half(hi), w1_half(lo), w1_half(hi), w2


def kernel(x, c, w_ada, b_ada, g_ffn1, w1_gate, w1_up, w1_down, g_mix, w_in, cmp_pos_k, cmp_w1_k, cmp_w2_k,
           cmp_pos_v, cmp_w1_v, cmp_w2_v, sinks, w_up_a, w_up_b, w_out, g_ffn2, w2_gate, w2_up, w2_down, g_final):
    bsz, seq, d = x.shape
    depth = w_ada.shape[0]
    nn_ranges, t_ranges, off_gm = _proj_column_ranges()
    h = x
    for l in range(depth):
        mod = _ada_call(c, w_ada[l], b_ada[l])
        sh1, sc1, gt1, sh2, sc2, gt2, sh3, sc3, gt3 = [m.reshape(bsz, 1, d) for m in jnp.split(mod, 9, axis=-1)]
        last = l == depth - 1

        h = _ffn_call(h, sh1, sc1, gt1, g_ffn1[l], w1_gate[l], w1_up[l], w1_down[l])

        w_nn = _take_columns(w_in[l], nn_ranges, NN_COLS).astype(BF16)
        w_t = _take_columns(w_in[l], t_ranges, T_ROWS).T.astype(BF16)
        kp, kvc_in, qn_t, qs_t, vt, gn_t = _proj_call(h, sh2, sc2, g_mix[l], w_nn, w_t)

        pa, pb, w1a, w1b, w2c = _compress_weights(cmp_pos_k[l], cmp_w1_k[l], cmp_w2_k[l],
                                                  cmp_pos_v[l], cmp_w1_v[l], cmp_w2_v[l])
        kvc, kvc_t = _cmp_call(kvc_in, pa, pb, w1a, w1b, w2c)

        ya, yb = _attn_call(sinks[l].reshape(-1), qn_t, qs_t, gn_t, kp, vt, kvc, kvc_t)

        h = _merge_call(h, sh2, sc2, gt2, g_mix[l], ya, yb,
                        w_in[l][:, off_gm:].astype(BF16), w_up_a[l].astype(BF16),
                        w_up_b[l].astype(BF16), w_out[l].astype(BF16))

        h = _ffn_call(h, sh3, sc3, gt3, g_ffn2[l], w2_gate[l], w2_up[l], w2_down[l],
                      g_final if last else None)
    return h
```

```python
import functools

import numpy as np
import jax
import jax.numpy as jnp
from jax import lax
from jax.experimental import pallas as pl
from jax.experimental.pallas import tpu as pltpu

F32 = jnp.float32
BF16 = jnp.bfloat16

HEAD_DIM = 64
N_HEADS = 8
N_KV = 2
N_REP = N_HEADS // N_KV
CMP_LEN = 32
CMP_STRIDE = 16
CMP_HIDDEN = 256
SEL_LEN = 64
SEL_TOPN = 8
NSA_WINDOW = 512
SWA_WINDOW = 128
FFN_RES = 0.5
RMS_EPS = 1e-6
NEG_INF = -1e30
SEL_BONUS = 1e4
ATTN_SCALE = HEAD_DIM ** -0.5

LANES = 128
VMEM_LIMIT = 56 * 1024 * 1024

SLOPES = [2.0 ** (-8.0 * (h + 1) / N_HEADS) for h in range(N_HEADS)]
LOG2E = 1.4426950408889634


def _bf16_terms(x, n=3):
    terms = []
    for _ in range(n):
        t = float(np.asarray(x, np.float32).astype(BF16).astype(np.float32))
        terms.append(t)
        x = x - t
    return terms


SLOPE_TERMS = [_bf16_terms(s * LOG2E) for s in SLOPES]


def _const_spec(shape):
    n = len(shape)
    return pl.BlockSpec(shape, lambda *_: (0,) * n, pipeline_mode=pl.Buffered(1))


def _params(sem):
    return pltpu.CompilerParams(dimension_semantics=sem, vmem_limit_bytes=VMEM_LIMIT)


def _modulated_norm(x, g, shift, scale):
    ms = jnp.mean(x * x, axis=-1, keepdims=True)
    y = x * lax.rsqrt(ms + RMS_EPS)
    return (y * g) * (1.0 + scale) + shift


def _split3(a):
    hi = a.astype(BF16)
    r1 = a - hi.astype(F32)
    mid = r1.astype(BF16)
    lo = (r1 - mid.astype(F32)).astype(BF16)
    return hi, mid, lo


def _ada_kernel(c_ref, w_ref, b_ref, o_ref):
    c = c_ref[...]
    a = c * jax.nn.sigmoid(c)
    a_hi = a.astype(BF16)
    a_lo = (a - a_hi.astype(F32)).astype(BF16)
    w = w_ref[...]
    w_hi = w.astype(BF16)
    w_lo = (w - w_hi.astype(F32)).astype(BF16)
    n = a.shape[0]
    both = jnp.dot(jnp.concatenate([a_hi, a_lo], axis=0), w_hi, preferred_element_type=F32)
    acc = both[:n] + jnp.dot(a_hi, w_lo, preferred_element_type=F32) + both[n:]
    o_ref[...] = acc + b_ref[...]


def _ada_call(c, w, b):
    bsz, d = c.shape
    n = w.shape[1]
    tn = 1024
    return pl.pallas_call(
        _ada_kernel,
        grid=(n // tn,),
        in_specs=[pl.BlockSpec((bsz, d), lambda j: (0, 0)),
                  pl.BlockSpec((d, tn), lambda j: (0, j)),
                  pl.BlockSpec((1, tn), lambda j: (0, j))],
        out_specs=pl.BlockSpec((bsz, tn), lambda j: (0, j)),
        out_shape=jax.ShapeDtypeStruct((bsz, n), F32),
        compiler_params=_params(("parallel",)),
        name="adaln",
    )(c, w, b.reshape(1, n))


def _ffn_kernel(x_ref, sh_ref, sc_ref, gt_ref, g_ref, wg_ref, wu_ref, wd_ref, *rest, tf, final, nsplit):
    o_ref = rest[-1]
    rows = x_ref.shape[1] // nsplit
    dff = wg_ref.shape[1]
    for h in range(nsplit):
        seg = slice(h * rows, (h + 1) * rows)
        x = x_ref[0, seg]
        u = _modulated_norm(x, g_ref[...], sh_ref[0], sc_ref[0]).astype(BF16)
        acc = None
        for c in range(dff // tf):
            cols = slice(c * tf, (c + 1) * tf)
            gate = jnp.dot(u, wg_ref[:, cols].astype(BF16), preferred_element_type=F32)
            up = jnp.dot(u, wu_ref[:, cols].astype(BF16), preferred_element_type=F32)
            act = (gate * jax.nn.sigmoid(gate) * up).astype(BF16)
            part = jnp.dot(act, wd_ref[cols, :].astype(BF16), preferred_element_type=F32)
            acc = part if acc is None else acc + part
        y = x + (FFN_RES * gt_ref[0]) * acc
        if final:
            gfin_ref = rest[0]
            ms = jnp.mean(y * y, axis=-1, keepdims=True)
            y = (y * lax.rsqrt(ms + RMS_EPS)) * gfin_ref[...]
        o_ref[0, seg] = y


def _ffn_call(h, shift, scale, gate, g, wg, wu, wd, g_final=None, *, tm=1024, tf=256, nsplit=2):
    bsz, s, d = h.shape
    dff = wg.shape[1]
    final = g_final is not None
    vec = pl.BlockSpec((1, 1, d), lambda b, i: (b, 0, 0))
    in_specs = [pl.BlockSpec((1, tm, d), lambda b, i: (b, i, 0)), vec, vec, vec,
                _const_spec((1, d)), _const_spec((d, dff)), _const_spec((d, dff)), _const_spec((dff, d))]
    args = [h, shift, scale, gate, g.reshape(1, d), wg, wu, wd]
    if final:
        in_specs.append(_const_spec((1, d)))
        args.append(g_final.reshape(1, d))
    return pl.pallas_call(
        functools.partial(_ffn_kernel, tf=tf, final=final, nsplit=nsplit),
        grid=(bsz, s // tm),
        in_specs=in_specs,
        out_specs=pl.BlockSpec((1, tm, d), lambda b, i: (b, i, 0)),
        out_shape=jax.ShapeDtypeStruct((bsz, s, d), F32),
        compiler_params=_params(("parallel", "parallel")),
        name="ffn_final" if final else "ffn",
    )(*args)


N_KPAIR = 3
Q_COLS = N_HEADS * HEAD_DIM
NN_COLS = (N_KPAIR + N_KV) * LANES
N_VSLOT = N_KPAIR * N_KV
GATE_ROWS = 32
VT_ROWS = N_VSLOT * HEAD_DIM
T_ROWS = 2 * Q_COLS + VT_ROWS + GATE_ROWS
KEY_CHUNK = LANES


def _proj_kernel(x_ref, sh_ref, sc_ref, g_ref, wn_ref, wt_ref, kp_ref, kvc_ref, qn_ref, qs_ref, vt_ref, gn_ref,
                 slab_scr, *, nsplit):
    rows = x_ref.shape[1] // nsplit

    def products(h):
        u = _modulated_norm(x_ref[0, h * rows:(h + 1) * rows], g_ref[...], sh_ref[0], sc_ref[0]).astype(BF16)
        return (jnp.dot(u, wn_ref[...], preferred_element_type=F32),
                lax.dot_general(wt_ref[...], u, (((1,), (1,)), ((), ())), preferred_element_type=F32))

    nxt = products(0)
    for h in range(nsplit):
        nn, tt = nxt
        if h + 1 < nsplit:
            nxt = products(h + 1)
        seg = slice(h * rows, (h + 1) * rows)
        for i in range(N_KPAIR):
            kp_ref[0, i, seg] = nn[:, i * LANES:(i + 1) * LANES].astype(BF16)
        crow = slice(h * rows // CMP_STRIDE, (h + 1) * rows // CMP_STRIDE)
        for i in range(N_KV):
            slab_scr[...] = nn[:, (N_KPAIR + i) * LANES:(N_KPAIR + i + 1) * LANES]
            for t in range(CMP_STRIDE):
                part = slab_scr[pl.ds(t, rows // CMP_STRIDE, stride=CMP_STRIDE), :]
                kvc_ref[0, i, crow, t * LANES:(t + 1) * LANES] = part.astype(BF16)
        qn_ref[0, :, seg] = (tt[0:Q_COLS] * (ATTN_SCALE * LOG2E)).astype(BF16)
        qs_ref[0, :, seg] = (tt[Q_COLS:2 * Q_COLS] * (ATTN_SCALE * LOG2E)).astype(BF16)
        base = 2 * Q_COLS
        for s in range(N_VSLOT):
            for c in range(rows // KEY_CHUNK):
                vt_ref[0, s, h * (rows // KEY_CHUNK) + c] = tt[base + s * HEAD_DIM: base + (s + 1) * HEAD_DIM,
                                                               c * KEY_CHUNK:(c + 1) * KEY_CHUNK].astype(BF16)
        gn_ref[0, :, seg] = tt[base + VT_ROWS:]


def _proj_call(h, shift, scale, g, wn, wt, *, tm=1024, nsplit=2):
    bsz, s, d = h.shape
    vec = pl.BlockSpec((1, 1, d), lambda b, i: (b, 0, 0))
    nck = tm // KEY_CHUNK
    return pl.pallas_call(
        functools.partial(_proj_kernel, nsplit=nsplit),
        grid=(bsz, s // tm),
        in_specs=[pl.BlockSpec((1, tm, d), lambda b, i: (b, i, 0)), vec, vec,
                  _const_spec((1, d)), _const_spec((d, NN_COLS)), _const_spec((T_ROWS, d))],
        out_specs=[pl.BlockSpec((1, N_KPAIR, tm, LANES), lambda b, i: (b, 0, i, 0)),
                   pl.BlockSpec((1, N_KV, tm // CMP_STRIDE, CMP_STRIDE * LANES), lambda b, i: (b, 0, i, 0)),
                   pl.BlockSpec((1, Q_COLS, tm), lambda b, i: (b, 0, i)),
                   pl.BlockSpec((1, Q_COLS, tm), lambda b, i: (b, 0, i)),
                   pl.BlockSpec((1, N_VSLOT, nck, HEAD_DIM, KEY_CHUNK), lambda b, i: (b, 0, i, 0, 0)),
                   pl.BlockSpec((1, GATE_ROWS, tm), lambda b, i: (b, 0, i))],
        out_shape=[jax.ShapeDtypeStruct((bsz, N_KPAIR, s, LANES), BF16),
                   jax.ShapeDtypeStruct((bsz, N_KV, s // CMP_STRIDE, CMP_STRIDE * LANES), BF16),
                   jax.ShapeDtypeStruct((bsz, Q_COLS, s), BF16),
                   jax.ShapeDtypeStruct((bsz, Q_COLS, s), BF16),
                   jax.ShapeDtypeStruct((bsz, N_VSLOT, s // KEY_CHUNK, HEAD_DIM, KEY_CHUNK), BF16),
                   jax.ShapeDtypeStruct((bsz, GATE_ROWS, s), F32)],
        scratch_shapes=[pltpu.VMEM((tm // nsplit, LANES), F32)],
        compiler_params=_params(("parallel", "parallel")),
        name="mixer_proj",
    )(h, shift, scale, g.reshape(1, d), wn, wt)


def _cmp_kernel(a_ref, pa_ref, pb_ref, w1a_ref, w1b_ref, w2_ref, o_ref, ot_ref):
    nb, ng, n, _ = a_ref.shape
    bias = (jnp.dot(pa_ref[...], w1a_ref[...], preferred_element_type=F32)
            + jnp.dot(pb_ref[...], w1b_ref[...], preferred_element_type=F32))[0:1]
    a = a_ref[...].reshape(nb * ng * n, a_ref.shape[3])
    first = jnp.dot(a, w1a_ref[...], preferred_element_type=F32)
    second = jnp.dot(a, w1b_ref[...], preferred_element_type=F32)
    for i in range(nb * ng):
        rows = slice(i * n, (i + 1) * n)
        hid = first[rows] + pltpu.roll(second[rows], n - 1, axis=0) + bias
        hid = jax.nn.gelu(hid)
        out = jnp.dot(hid.astype(BF16), w2_ref[...], preferred_element_type=F32)
        o_ref[i // ng, i % ng] = out.astype(BF16)
        ot_ref[i // ng, i % ng] = out.T.astype(BF16)


def _cmp_call(kv_chunks, pa, pb, w1a, w1b, w2):
    bsz, _, nchunk, width = kv_chunks.shape
    nb = 2 if bsz % 2 == 0 else 1
    return pl.pallas_call(
        _cmp_kernel,
        grid=(bsz // nb,),
        in_specs=[pl.BlockSpec((nb, N_KV, nchunk, width), lambda b: (b, 0, 0, 0)),
                  _const_spec(pa.shape), _const_spec(pb.shape),
                  _const_spec(w1a.shape), _const_spec(w1b.shape), _const_spec(w2.shape)],
        out_specs=[pl.BlockSpec((nb, N_KV, nchunk, LANES), lambda b: (b, 0, 0, 0)),
                   pl.BlockSpec((nb, N_KV, LANES, nchunk), lambda b: (b, 0, 0, 0))],
        out_shape=[jax.ShapeDtypeStruct((bsz, N_KV, nchunk, LANES), BF16),
                   jax.ShapeDtypeStruct((bsz, N_KV, LANES, nchunk), BF16)],
        compiler_params=_params(("parallel",)),
        name="nsa_compress",
    )(kv_chunks, pa, pb, w1a, w1b, w2)


POS_HI, POS_LO, POS_ONE = 96, 99, 102
MAX_SEL_BLOCKS = POS_HI


def _position_tables(seq):
    key = np.arange(seq)
    kpos = np.zeros((seq, LANES), np.float32)
    kpos[key, key // SEL_LEN] = 1.0
    kpos[:, POS_HI:POS_HI + 3] = ((key // SEL_LEN) * SEL_LEN)[:, None]
    kpos[:, POS_LO:POS_LO + 3] = (key % SEL_LEN)[:, None]
    kpos[:, POS_ONE:POS_ONE + 3] = 1.0
    ncmp = seq // CMP_STRIDE
    cpos = np.zeros((ncmp, LANES), np.float32)
    cpos[:, POS_HI:POS_HI + 3] = (np.arange(ncmp) * CMP_STRIDE)[:, None]
    cpos[:, POS_LO:POS_LO + 3] = CMP_LEN - 1
    cpos[:, POS_ONE:POS_ONE + 3] = 1.0
    return jnp.asarray(kpos, BF16), jnp.asarray(cpos, BF16)


def _attn_kernel(sinks_ref, qn_ref, qs_ref, gn_ref, kp_ref, vt_ref, kvc_ref, kvct_ref, kpos_ref, cpos_ref,
                 ya_ref, yb_ref, s_scr, m_scr, acc_scr, blocks_ref, *, seq, tq, tqs):
    qi = pl.program_id(1)
    t0 = qi * tq
    cols = N_REP * tq
    nsel = seq // SEL_LEN
    ncmp = kvc_ref.shape[2]
    n_top = min(SEL_TOPN, nsel)

    def stacked_q(q_ref, g, low, width, lane0=0, sel_t=None):
        sub = lax.broadcasted_iota(jnp.int32, (LANES, width), 0)
        zero_q = jnp.zeros((HEAD_DIM, width), BF16)
        blocks = []
        for r in range(N_REP):
            h = g * N_REP + r
            q = q_ref[0, h * HEAD_DIM:(h + 1) * HEAD_DIM, lane0:lane0 + width]
            qpart = jnp.concatenate([q, zero_q] if low else [zero_q, q], axis=0)
            ext = jnp.zeros((LANES, width), F32)
            for i, c in enumerate(SLOPE_TERMS[h]):
                ext = jnp.where((sub == POS_HI + i) | (sub == POS_LO + i), c, ext)
            tpos = (t0 + lane0 + lax.broadcasted_iota(jnp.int32, (1, width), 1)).astype(F32)
            for i, part in enumerate(_split3(-(SLOPES[h] * LOG2E) * tpos)):
                ext = jnp.where(sub == POS_ONE + i, part.astype(F32), ext)
            if sel_t is not None:
                ext = ext + jnp.where(sub < nsel, (sel_t - 1.0) * (-NEG_INF), 0.0)
            blocks.append(jnp.concatenate([qpart, ext.astype(BF16)], axis=0))
        return jnp.concatenate(blocks, axis=1)

    def masked(s, mask, width):
        return jnp.concatenate(
            [jnp.where(mask, s[:, r * width:(r + 1) * width], NEG_INF) for r in range(N_REP)], axis=1)

    def biased(s, bias, width):
        sb = s.astype(BF16)
        return jnp.concatenate([sb[:, r * width:(r + 1) * width] + bias for r in range(N_REP)], axis=1)

    def value_rows(vt):
        return jnp.concatenate([jnp.ones_like(vt), vt], axis=0)

    def load_vt(slot, first_chunk, nchunks):
        return jnp.concatenate([vt_ref[0, slot, first_chunk + j] for j in range(nchunks)], axis=1)

    def band_geometry(window, sub):
        span = tqs + -(-window // KEY_CHUNK) * KEY_CHUNK
        ts = t0 + sub * tqs
        k_start = pl.multiple_of(jnp.maximum(ts + tqs - span, 0), KEY_CHUNK)
        d = ((ts - k_start) + lax.broadcasted_iota(jnp.int32, (span, tqs), 1)
             - lax.broadcasted_iota(jnp.int32, (span, tqs), 0))
        return span, k_start, jnp.where((d >= 0) & (d < window), 0.0, NEG_INF).astype(BF16)

    def band_scores(qt, pair, geometry):
        span, k_start, bias = geometry
        kaug = jnp.concatenate([kp_ref[0, pair, pl.ds(k_start, span), :], kpos_ref[pl.ds(k_start, span), :]], axis=1)
        return biased(jnp.dot(kaug, qt, preferred_element_type=F32), bias, tqs)

    def band_output(s, vslot, geometry, sink_row=None):
        span, k_start, _ = geometry
        m = jnp.max(s, axis=0, keepdims=True)
        if sink_row is not None:
            m = jnp.maximum(m, sink_row)
        p = jnp.exp2(s - m)
        vt = load_vt(vslot, k_start // KEY_CHUNK, span // KEY_CHUNK)
        o = jnp.dot(value_rows(vt), p, preferred_element_type=F32)
        l = o[:HEAD_DIM]
        if sink_row is not None:
            l = l + jnp.exp2(sink_row - m).astype(F32)
        return o[HEAD_DIM:] / l

    tc = t0 + lax.broadcasted_iota(jnp.int32, (ncmp, tq), 1)
    cend = lax.broadcasted_iota(jnp.int32, (ncmp, tq), 0) * CMP_STRIDE + (CMP_LEN - 1)
    cmask = tc >= cend
    row_ok = (t0 + lax.broadcasted_iota(jnp.int32, (1, tq), 1)) >= (CMP_LEN - 1)
    row_ok = jnp.concatenate([row_ok.astype(F32)] * N_REP, axis=1)
    jn = lax.broadcasted_iota(jnp.int32, (nsel, ncmp), 0) * SEL_LEN
    cn = lax.broadcasted_iota(jnp.int32, (nsel, ncmp), 1) * CMP_STRIDE
    ov_t = jnp.where((cn < jn + SEL_LEN) & (cn + CMP_LEN > jn), 1.0, 0.0).astype(BF16)
    jb = lax.broadcasted_iota(jnp.int32, (nsel, tq), 0)
    cur = (t0 + lax.broadcasted_iota(jnp.int32, (nsel, tq), 1)) // SEL_LEN
    forced = (jb == 0) | (jb == cur) | (jb == cur - 1)
    valid = jb <= cur
    cmp_scores = [jnp.dot(jnp.concatenate([kvc_ref[0, g], cpos_ref[...]], axis=1), stacked_q(qn_ref, g, True, tq),
                          preferred_element_type=F32) for g in range(N_KV)]
    o_cmp, scores_sel = [], []
    for g in range(N_KV):
        s = masked(cmp_scores[g], cmask, tq)
        e = jnp.exp2(s - jnp.max(s, axis=0, keepdims=True))
        p = e * (row_ok / jnp.sum(e, axis=0, keepdims=True))
        o_cmp.append(jnp.dot(value_rows(kvct_ref[0, g, HEAD_DIM:, :]), p.astype(BF16),
                             preferred_element_type=F32)[HEAD_DIM:])
        psum = p[:, 0:tq]
        for r in range(1, N_REP):
            psum = psum + p[:, r * tq:(r + 1) * tq]
        imp_t = sum(jnp.dot(ov_t, part, preferred_element_type=F32) for part in _split3(psum))
        scores_sel.append(jnp.where(forced, SEL_BONUS, jnp.where(valid, imp_t, -1.0)))

    items = []
    for sub in range(tq // tqs):
        geo_win = band_geometry(NSA_WINDOW, sub)
        geo_swa = band_geometry(SWA_WINDOW, sub)
        for g in range(N_KV):
            sink_row = jnp.concatenate(
                [jnp.full((1, tqs), sinks_ref[g * N_REP + r] * LOG2E, F32) for r in range(N_REP)],
                axis=1).astype(BF16)
            items.append((qn_ref, g, sub, 1, N_KV + g, geo_win, None))
            items.append((qs_ref, g, sub, 2, 2 * N_KV + g, geo_swa, sink_row))

    def item_scores(item):
        q_ref, g, sub, pair, _, geo, _ = item
        return band_scores(stacked_q(q_ref, g, g == 0, tqs, sub * tqs), pair, geo)

    outs, ahead = [], 2
    pending = [item_scores(items[k]) for k in range(ahead)]
    for k, item in enumerate(items):
        if k + ahead < len(items):
            pending.append(item_scores(items[k + ahead]))
        outs.append(band_output(pending.pop(0), item[4], item[5], item[6]))

    gsig = jax.nn.sigmoid(gn_ref[0])
    partial, yb_cols = [], []
    for sub in range(tq // tqs):
        lane0 = sub * tqs
        gates = gsig[:, lane0:lane0 + tqs]
        rows, yb_rows = [], []
        for g in range(N_KV):
            o_win, o_swa = outs[(sub * N_KV + g) * 2], outs[(sub * N_KV + g) * 2 + 1]
            for r in range(N_REP):
                c = (g * N_REP + r) * 3
                wide = slice(r * tq + lane0, r * tq + lane0 + tqs)
                rows.append(gates[c:c + 1] * o_cmp[g][:, wide] + gates[c + 2:c + 3] * o_win[:, r * tqs:(r + 1) * tqs])
            yb_rows += [o_swa[:, r * tqs:(r + 1) * tqs] for r in range(N_REP)]
        partial.append(rows)
        yb_cols.append(jnp.concatenate(yb_rows, axis=0))
    yb_ref[0] = jnp.concatenate(yb_cols, axis=1).astype(BF16)

    qt_sel, chosen = [], None
    for score in scores_sel:
        rank = [jnp.zeros((8, tq), F32) for _ in range(nsel // 8)]
        for i in range(nsel):
            row = score[i:i + 1, :]
            for k in range(nsel // 8):
                blk = score[8 * k:8 * (k + 1)]
                ge = jnp.where(row >= blk, 1.0, 0.0)
                gt = jnp.where(row > blk, 1.0, 0.0)
                if 8 * k > i:
                    beats = ge
                elif 8 * k + 7 <= i:
                    beats = gt
                else:
                    beats = jnp.where(lax.broadcasted_iota(jnp.int32, (8, tq), 0) > i - 8 * k, ge, gt)
                rank[k] = rank[k] + beats
        sel_t = jnp.where(jnp.concatenate(rank, axis=0) < n_top, 1.0, 0.0)
        sel_t = jnp.concatenate([sel_t, jnp.zeros((LANES - nsel, tq), F32)], axis=0)
        chosen = sel_t if chosen is None else chosen + sel_t
        qt_sel.append(stacked_q(qn_ref, len(qt_sel), len(qt_sel) == 0, tq, 0, sel_t))

    nck = tq // KEY_CHUNK
    per_blk = tq // SEL_LEN
    n_prev = jnp.int32(0)
    for kb in range(seq // tq - 1):
        needed = (jnp.max(chosen[kb * per_blk:(kb + 1) * per_blk, :]) > 0.5) & (kb < qi)
        blocks_ref[n_prev] = kb
        n_prev = n_prev + needed.astype(jnp.int32)
    blocks_ref[n_prev] = qi

    def key_rows(ids):
        starts = [pl.multiple_of(kb * tq, tq) for kb in ids]
        return jnp.concatenate([jnp.concatenate([kp_ref[0, 0, pl.ds(k0, tq), :] for k0 in starts], axis=0),
                                jnp.concatenate([kpos_ref[pl.ds(k0, tq), :] for k0 in starts], axis=0)], axis=1)

    def score_step(ids, slot, bias=None):
        n = len(ids)
        kaug = key_rows(ids)
        raw = [jnp.dot(kaug, qt_sel[g], preferred_element_type=F32) for g in range(N_KV)]
        for g in range(N_KV):
            s = raw[g].astype(BF16) if bias is None else biased(raw[g], bias, tq)
            s_scr[g, pl.ds(slot, n)] = s.reshape(n, tq, cols)
            m_scr[g] = jnp.maximum(m_scr[g], jnp.max(s.reshape(n * tq // 16, 16, cols), axis=0))

    def value_step(ids, slot):
        n = len(ids)
        ps = [jnp.exp2(s_scr[g, pl.ds(slot, n)].reshape(n * tq, cols) - m_scr[g, 0:1, :]) for g in range(N_KV)]
        for g in range(N_KV):
            vt = jnp.concatenate([load_vt(g, kb * nck, nck) for kb in ids], axis=1)
            acc_scr[g] += jnp.dot(value_rows(vt), ps[g], preferred_element_type=F32)

    for g in range(N_KV):
        m_scr[g] = jnp.full((16, cols), NEG_INF, BF16)
        acc_scr[g] = jnp.zeros((LANES, cols), F32)

    def pass1(i, carry):
        score_step((blocks_ref[2 * i], blocks_ref[2 * i + 1]), 2 * i)
        return carry

    lax.fori_loop(0, n_prev // 2, pass1, 0)
    causal = jnp.where(lax.broadcasted_iota(jnp.int32, (tq, tq), 1) >= lax.broadcasted_iota(jnp.int32, (tq, tq), 0),
                       0.0, NEG_INF).astype(BF16)
    odd = (n_prev % 2) == 1

    @pl.when(odd)
    def _():
        score_step((blocks_ref[n_prev - 1], qi), n_prev - 1,
                   jnp.concatenate([jnp.zeros((tq, tq), BF16), causal], axis=0))

    @pl.when(jnp.logical_not(odd))
    def _():
        score_step((qi,), n_prev, causal)

    for g in range(N_KV):
        m_scr[g] = jnp.broadcast_to(jnp.max(m_scr[g], axis=0, keepdims=True), (16, cols))

    def pass2(i, carry):
        value_step((blocks_ref[2 * i], blocks_ref[2 * i + 1]), 2 * i)
        return carry

    lax.fori_loop(0, (n_prev + 1) // 2, pass2, 0)

    @pl.when(jnp.logical_not(odd))
    def _():
        value_step((qi,), n_prev)

    ya_cols = []
    for sub in range(tq // tqs):
        lane0 = sub * tqs
        rows = []
        for g in range(N_KV):
            o_slc = acc_scr[g, HEAD_DIM:, :] / acc_scr[g, :HEAD_DIM, :]
            for r in range(N_REP):
                c = (g * N_REP + r) * 3 + 1
                wide = slice(r * tq + lane0, r * tq + lane0 + tqs)
                rows.append(partial[sub][g * N_REP + r] + gsig[c:c + 1, lane0:lane0 + tqs] * o_slc[:, wide])
        ya_cols.append(jnp.concatenate(rows, axis=0))
    ya_ref[0] = jnp.concatenate(ya_cols, axis=1).astype(BF16)


def _attn_call(sinks, qn_t, qs_t, gn_t, kp, vt, kvc, kvc_t, *, tq=256, tqs=128):
    bsz, _, s = qn_t.shape
    ncmp = kvc.shape[2]
    assert s % tq == 0 and s // SEL_LEN <= MAX_SEL_BLOCKS and s >= tq + NSA_WINDOW
    kpos, cpos = _position_tables(s)
    qspec = pl.BlockSpec((1, Q_COLS, tq), lambda b, i: (b, 0, i))
    yspec = pl.BlockSpec((1, Q_COLS, tq), lambda b, i: (b, 0, i))
    cols = N_REP * tq
    return pl.pallas_call(
        functools.partial(_attn_kernel, seq=s, tq=tq, tqs=tqs),
        grid=(bsz, s // tq),
        in_specs=[pl.BlockSpec(memory_space=pltpu.SMEM),
                  qspec, qspec,
                  pl.BlockSpec((1, GATE_ROWS, tq), lambda b, i: (b, 0, i)),
                  pl.BlockSpec((1, N_KPAIR, s, LANES), lambda b, i: (b, 0, 0, 0)),
                  pl.BlockSpec((1, N_VSLOT, s // KEY_CHUNK, HEAD_DIM, KEY_CHUNK), lambda b, i: (b, 0, 0, 0, 0)),
                  pl.BlockSpec((1, N_KV, ncmp, LANES), lambda b, i: (b, 0, 0, 0)),
                  pl.BlockSpec((1, N_KV, LANES, ncmp), lambda b, i: (b, 0, 0, 0)),
                  _const_spec((s, LANES)), _const_spec((ncmp, LANES))],
        out_specs=[yspec, yspec],
        out_shape=[jax.ShapeDtypeStruct((bsz, Q_COLS, s), BF16),
                   jax.ShapeDtypeStruct((bsz, Q_COLS, s), BF16)],
        scratch_shapes=[pltpu.VMEM((N_KV, s // tq, tq, cols), BF16),
                        pltpu.VMEM((N_KV, 16, cols), BF16),
                        pltpu.VMEM((N_KV, LANES, cols), F32),
                        pltpu.SMEM((s // tq + 1,), jnp.int32)],
        compiler_params=_params(("parallel", "arbitrary")),
        name="hybrid_attention",
    )(sinks, qn_t, qs_t, gn_t, kp, vt, kvc, kvc_t, kpos, cpos)


def _merge_kernel(x_ref, sh_ref, sc_ref, gt_ref, g_ref, ya_ref, yb_ref, wgm_ref, wa_ref, wb_ref, wo_ref, o_ref, *,
                  nsplit):
    d = x_ref.shape[-1]
    rows = x_ref.shape[1] // nsplit

    def products(h):
        seg = slice(h * rows, (h + 1) * rows)
        u = _modulated_norm(x_ref[0, seg], g_ref[...], sh_ref[0], sc_ref[0]).astype(BF16)
        tn = (((0,), (0,)), ((), ()))
        return (lax.dot_general(ya_ref[0, :, seg], wa_ref[...].astype(BF16), tn, preferred_element_type=F32),
                jnp.dot(u, wgm_ref[:, :d], preferred_element_type=F32),
                lax.dot_general(yb_ref[0, :, seg], wb_ref[...].astype(BF16), tn, preferred_element_type=F32),
                jnp.dot(u, wgm_ref[:, d:], preferred_element_type=F32))

    nxt = products(0)
    for h in range(nsplit):
        up_a, gate_a, up_b, gate_b = nxt
        if h + 1 < nsplit:
            nxt = products(h + 1)
        merged = jax.nn.sigmoid(gate_a) * up_a + jax.nn.sigmoid(gate_b) * up_b
        y = jnp.dot(merged.astype(BF16), wo_ref[...].astype(BF16), preferred_element_type=F32)
        seg = slice(h * rows, (h + 1) * rows)
        o_ref[0, seg] = x_ref[0, seg] + gt_ref[0] * y


def _merge_call(h, shift, scale, gate, g, ya, yb, wgm, wa, wb, wo, *, tm=1024, nsplit=4):
    bsz, s, d = h.shape
    vec = pl.BlockSpec((1, 1, d), lambda b, i: (b, 0, 0))
    tok = pl.BlockSpec((1, tm, d), lambda b, i: (b, i, 0))
    ysp = pl.BlockSpec((1, Q_COLS, tm), lambda b, i: (b, 0, i))
    return pl.pallas_call(
        functools.partial(_merge_kernel, nsplit=nsplit),
        grid=(bsz, s // tm),
        in_specs=[tok, vec, vec, vec, _const_spec((1, d)), ysp, ysp,
                  _const_spec(wgm.shape), _const_spec(wa.shape), _const_spec(wb.shape), _const_spec(wo.shape)],
        out_specs=tok,
        out_shape=jax.ShapeDtypeStruct((bsz, s, d), F32),
        compiler_params=_params(("parallel", "parallel")),
        name="mixer_merge",
    )(h, shift, scale, gate, g.reshape(1, d), ya, yb, wgm, wa, wb, wo)


def _proj_column_ranges():
    kvw = 2 * N_KV * HEAD_DIM
    off_qn = 0
    off_c = off_qn + Q_COLS
    off_s = off_c + kvw
    off_w = off_s + kvw
    off_gn = off_w + kvw
    off_qs = off_gn + 3 * N_HEADS
    off_b = off_qs + Q_COLS
    off_gm = off_b + kvw
    half = N_KV * HEAD_DIM
    nn = [(off, off + half) for off in (off_s, off_w, off_b)]
    for g in range(N_KV):
        nn += [(off_c + g * HEAD_DIM, off_c + (g + 1) * HEAD_DIM),
               (off_c + half + g * HEAD_DIM, off_c + half + (g + 1) * HEAD_DIM)]
    tt = [(off_qn, off_qn + Q_COLS), (off_qs, off_qs + Q_COLS)]
    tt += [(off + half, off + 2 * half) for off in (off_s, off_w, off_b)]
    tt += [(off_gn, off_gn + 3 * N_HEADS)]
    return nn, tt, off_gm


def _take_columns(w, ranges, width):
    parts = [w[:, a:b] for a, b in ranges]
    have = sum(b - a for a, b in ranges)
    if width > have:
        parts.append(jnp.zeros((w.shape[0], width - have), w.dtype))
    return jnp.concatenate(parts, axis=1)


def _compress_weights(pos_k, w1_k, w2_k, pos_v, w1_v, w2_v):
    half = CMP_LEN // 2
    zk = jnp.zeros((half, HEAD_DIM, CMP_HIDDEN), F32)

    def w1_half(sl):
        wk = jnp.concatenate([w1_k[sl], zk], axis=-1)
        wv = jnp.concatenate([zk, w1_v[sl]], axis=-1)
        return jnp.concatenate([wk, wv], axis=1).reshape(half * 2 * HEAD_DIM, 2 * CMP_HIDDEN).astype(BF16)

    def pos_half(sl):
        p = jnp.concatenate([pos_k[sl], pos_v[sl]], axis=1).reshape(1, half * 2 * HEAD_DIM)
        return jnp.broadcast_to(p, (8, p.shape[1])).astype(BF16)

    z2 = jnp.zeros((CMP_HIDDEN, HEAD_DIM), F32)
    w2 = jnp.concatenate([jnp.concatenate([w2_k, z2], axis=1),
                          jnp.concatenate([z2, w2_v], axis=1)], axis=0).astype(BF16)
    lo, hi = slice(0, half), slice(half, CMP_LEN)
    return pos_half(lo), pos_half(hi), w1_half(lo), w1_half(hi), w2


def kernel(x, c, w_ada, b_ada, g_ffn1, w1_gate, w1_up, w1_down, g_mix, w_in, cmp_pos_k, cmp_w1_k, cmp_w2_k,
           cmp_pos_v, cmp_w1_v, cmp_w2_v, sinks, w_up_a, w_up_b, w_out, g_ffn2, w2_gate, w2_up, w2_down, g_final):
    bsz, seq, d = x.shape
    depth = w_ada.shape[0]
    nn_ranges, t_ranges, off_gm = _proj_column_ranges()
    h = x
    for l in range(depth):
        mod = _ada_call(c, w_ada[l], b_ada[l])
        sh1, sc1, gt1, sh2, sc2, gt2, sh3, sc3, gt3 = [m.reshape(bsz, 1, d) for m in jnp.split(mod, 9, axis=-1)]
        last = l == depth - 1

        h = _ffn_call(h, sh1, sc1, gt1, g_ffn1[l], w1_gate[l], w1_up[l], w1_down[l])

        w_nn = _take_columns(w_in[l], nn_ranges, NN_COLS).astype(BF16)
        w_t = _take_columns(w_in[l], t_ranges, T_ROWS).T.astype(BF16)
        kp, kvc_in, qn_t, qs_t, vt, gn_t = _proj_call(h, sh2, sc2, g_mix[l], w_nn, w_t)

        pa, pb, w1a, w1b, w2c = _compress_weights(cmp_pos_k[l], cmp_w1_k[l], cmp_w2_k[l],
                                                  cmp_pos_v[l], cmp_w1_v[l], cmp_w2_v[l])
        kvc, kvc_t = _cmp_call(kvc_in, pa, pb, w1a, w1b, w2c)

        ya, yb = _attn_call(sinks[l].reshape(-1), qn_t, qs_t, gn_t, kp, vt, kvc, kvc_t)

        h = _merge_call(h, sh2, sc2, gt2, g_mix[l], ya, yb,
                        w_in[l][:, off_gm:].astype(BF16), w_up_a[l], w_up_b[l], w_out[l])

        h = _ffn_call(h, sh3, sc3, gt3, g_ffn2[l], w2_gate[l], w2_up[l], w2_down[l],
                      g_final if last else None)
    return h
```

```python
import functools

import numpy as np
import jax
import jax.numpy as jnp
from jax import lax
from jax.experimental import pallas as pl
from jax.experimental.pallas import tpu as pltpu

F32 = jnp.float32
BF16 = jnp.bfloat16

HEAD_DIM = 64
N_HEADS = 8
N_KV = 2
N_REP = N_HEADS // N_KV
CMP_LEN = 32
CMP_STRIDE = 16
CMP_HIDDEN = 256
SEL_LEN = 64
SEL_TOPN = 8
NSA_WINDOW = 512
SWA_WINDOW = 128
FFN_RES = 0.5
RMS_EPS = 1e-6
NEG_INF = -1e30
SEL_BONUS = 1e4
ATTN_SCALE = HEAD_DIM ** -0.5

LANES = 128
VMEM_LIMIT = 56 * 1024 * 1024

SLOPES = [2.0 ** (-8.0 * (h + 1) / N_HEADS) for h in range(N_HEADS)]
LOG2E = 1.4426950408889634


def _bf16_terms(x, n=3):
    terms = []
    for _ in range(n):
        t = float(np.asarray(x, np.float32).astype(BF16).astype(np.float32))
        terms.append(t)
        x = x - t
    return terms


SLOPE_TERMS = [_bf16_terms(s * LOG2E) for s in SLOPES]


def _const_spec(shape):
    n = len(shape)
    return pl.BlockSpec(shape, lambda *_: (0,) * n, pipeline_mode=pl.Buffered(1))


def _params(sem):
    return pltpu.CompilerParams(dimension_semantics=sem, vmem_limit_bytes=VMEM_LIMIT)


def _modulated_norm(x, g, shift, scale):
    ms = jnp.mean(x * x, axis=-1, keepdims=True)
    y = x * lax.rsqrt(ms + RMS_EPS)
    return (y * g) * (1.0 + scale) + shift


def _split3(a):
    hi = a.astype(BF16)
    r1 = a - hi.astype(F32)
    mid = r1.astype(BF16)
    lo = (r1 - mid.astype(F32)).astype(BF16)
    return hi, mid, lo


def _ada_kernel(c_ref, w_ref, b_ref, o_ref):
    c = c_ref[...]
    a = c * jax.nn.sigmoid(c)
    a_hi = a.astype(BF16)
    a_lo = (a - a_hi.astype(F32)).astype(BF16)
    w = w_ref[...]
    w_hi = w.astype(BF16)
    w_lo = (w - w_hi.astype(F32)).astype(BF16)
    n = a.shape[0]
    both = jnp.dot(jnp.concatenate([a_hi, a_lo], axis=0), w_hi, preferred_element_type=F32)
    acc = both[:n] + jnp.dot(a_hi, w_lo, preferred_element_type=F32) + both[n:]
    o_ref[...] = acc + b_ref[...]


def _ada_call(c, w, b):
    bsz, d = c.shape
    n = w.shape[1]
    tn = 1024
    return pl.pallas_call(
        _ada_kernel,
        grid=(n // tn,),
        in_specs=[pl.BlockSpec((bsz, d), lambda j: (0, 0)),
                  pl.BlockSpec((d, tn), lambda j: (0, j)),
                  pl.BlockSpec((1, tn), lambda j: (0, j))],
        out_specs=pl.BlockSpec((bsz, tn), lambda j: (0, j)),
        out_shape=jax.ShapeDtypeStruct((bsz, n), F32),
        compiler_params=_params(("parallel",)),
        name="adaln",
    )(c, w, b.reshape(1, n))


def _ffn_kernel(x_ref, sh_ref, sc_ref, gt_ref, g_ref, wg_ref, wu_ref, wd_ref, *rest, tf, final, nsplit):
    o_ref = rest[-1]
    rows = x_ref.shape[1] // nsplit
    dff = wg_ref.shape[1]
    for h in range(nsplit):
        seg = slice(h * rows, (h + 1) * rows)
        x = x_ref[0, seg]
        u = _modulated_norm(x, g_ref[...], sh_ref[0], sc_ref[0]).astype(BF16)
        acc = None
        for c in range(dff // tf):
            cols = slice(c * tf, (c + 1) * tf)
            gate = jnp.dot(u, wg_ref[:, cols].astype(BF16), preferred_element_type=F32)
            up = jnp.dot(u, wu_ref[:, cols].astype(BF16), preferred_element_type=F32)
            act = (gate * jax.nn.sigmoid(gate) * up).astype(BF16)
            part = jnp.dot(act, wd_ref[cols, :].astype(BF16), preferred_element_type=F32)
            acc = part if acc is None else acc + part
        y = x + (FFN_RES * gt_ref[0]) * acc
        if final:
            gfin_ref = rest[0]
            ms = jnp.mean(y * y, axis=-1, keepdims=True)
            y = (y * lax.rsqrt(ms + RMS_EPS)) * gfin_ref[...]
        o_ref[0, seg] = y


def _ffn_call(h, shift, scale, gate, g, wg, wu, wd, g_final=None, *, tm=1024, tf=256, nsplit=2):
    bsz, s, d = h.shape
    dff = wg.shape[1]
    final = g_final is not None
    vec = pl.BlockSpec((1, 1, d), lambda b, i: (b, 0, 0))
    in_specs = [pl.BlockSpec((1, tm, d), lambda b, i: (b, i, 0)), vec, vec, vec,
                _const_spec((1, d)), _const_spec((d, dff)), _const_spec((d, dff)), _const_spec((dff, d))]
    args = [h, shift, scale, gate, g.reshape(1, d), wg, wu, wd]
    if final:
        in_specs.append(_const_spec((1, d)))
        args.append(g_final.reshape(1, d))
    return pl.pallas_call(
        functools.partial(_ffn_kernel, tf=tf, final=final, nsplit=nsplit),
        grid=(bsz, s // tm),
        in_specs=in_specs,
        out_specs=pl.BlockSpec((1, tm, d), lambda b, i: (b, i, 0)),
        out_shape=jax.ShapeDtypeStruct((bsz, s, d), F32),
        compiler_params=_params(("parallel", "parallel")),
        name="ffn_final" if final else "ffn",
    )(*args)


N_KPAIR = 3
Q_COLS = N_HEADS * HEAD_DIM
NN_COLS = (N_KPAIR + N_KV) * LANES
N_VSLOT = N_KPAIR * N_KV
GATE_ROWS = 32
VT_ROWS = N_VSLOT * HEAD_DIM
T_ROWS = 2 * Q_COLS + VT_ROWS + GATE_ROWS
KEY_CHUNK = LANES


def _proj_kernel(x_ref, sh_ref, sc_ref, g_ref, wn_ref, wt_ref, kp_ref, kvc_ref, qn_ref, qs_ref, vt_ref, gn_ref,
                 slab_scr, *, nsplit):
    rows = x_ref.shape[1] // nsplit

    def products(h):
        u = _modulated_norm(x_ref[0, h * rows:(h + 1) * rows], g_ref[...], sh_ref[0], sc_ref[0]).astype(BF16)
        return (jnp.dot(u, wn_ref[...], preferred_element_type=F32),
                lax.dot_general(wt_ref[...], u, (((1,), (1,)), ((), ())), preferred_element_type=F32))

    nxt = products(0)
    for h in range(nsplit):
        nn, tt = nxt
        if h + 1 < nsplit:
            nxt = products(h + 1)
        seg = slice(h * rows, (h + 1) * rows)
        for i in range(N_KPAIR):
            kp_ref[0, i, seg] = nn[:, i * LANES:(i + 1) * LANES].astype(BF16)
        crow = slice(h * rows // CMP_STRIDE, (h + 1) * rows // CMP_STRIDE)
        for i in range(N_KV):
            slab_scr[...] = nn[:, (N_KPAIR + i) * LANES:(N_KPAIR + i + 1) * LANES]
            for t in range(CMP_STRIDE):
                part = slab_scr[pl.ds(t, rows // CMP_STRIDE, stride=CMP_STRIDE), :]
                kvc_ref[0, i, crow, t * LANES:(t + 1) * LANES] = part.astype(BF16)
        qn_ref[0, :, seg] = (tt[0:Q_COLS] * (ATTN_SCALE * LOG2E)).astype(BF16)
        qs_ref[0, :, seg] = (tt[Q_COLS:2 * Q_COLS] * (ATTN_SCALE * LOG2E)).astype(BF16)
        base = 2 * Q_COLS
        for s in range(N_VSLOT):
            for c in range(rows // KEY_CHUNK):
                vt_ref[0, s, h * (rows // KEY_CHUNK) + c] = tt[base + s * HEAD_DIM: base + (s + 1) * HEAD_DIM,
                                                               c * KEY_CHUNK:(c + 1) * KEY_CHUNK].astype(BF16)
        gn_ref[0, :, seg] = tt[base + VT_ROWS:]


def _proj_call(h, shift, scale, g, wn, wt, *, tm=2048, nsplit=4):
    bsz, s, d = h.shape
    tm = min(tm, s)
    assert s % tm == 0 and (tm // nsplit) % (CMP_STRIDE * 8) == 0
    vec = pl.BlockSpec((1, 1, d), lambda b, i: (b, 0, 0))
    nck = tm // KEY_CHUNK
    return pl.pallas_call(
        functools.partial(_proj_kernel, nsplit=nsplit),
        grid=(bsz, s // tm),
        in_specs=[pl.BlockSpec((1, tm, d), lambda b, i: (b, i, 0)), vec, vec,
                  _const_spec((1, d)), _const_spec((d, NN_COLS)), _const_spec((T_ROWS, d))],
        out_specs=[pl.BlockSpec((1, N_KPAIR, tm, LANES), lambda b, i: (b, 0, i, 0)),
                   pl.BlockSpec((1, N_KV, tm // CMP_STRIDE, CMP_STRIDE * LANES), lambda b, i: (b, 0, i, 0)),
                   pl.BlockSpec((1, Q_COLS, tm), lambda b, i: (b, 0, i)),
                   pl.BlockSpec((1, Q_COLS, tm), lambda b, i: (b, 0, i)),
                   pl.BlockSpec((1, N_VSLOT, nck, HEAD_DIM, KEY_CHUNK), lambda b, i: (b, 0, i, 0, 0)),
                   pl.BlockSpec((1, GATE_ROWS, tm), lambda b, i: (b, 0, i))],
        out_shape=[jax.ShapeDtypeStruct((bsz, N_KPAIR, s, LANES), BF16),
                   jax.ShapeDtypeStruct((bsz, N_KV, s // CMP_STRIDE, CMP_STRIDE * LANES), BF16),
                   jax.ShapeDtypeStruct((bsz, Q_COLS, s), BF16),
                   jax.ShapeDtypeStruct((bsz, Q_COLS, s), BF16),
                   jax.ShapeDtypeStruct((bsz, N_VSLOT, s // KEY_CHUNK, HEAD_DIM, KEY_CHUNK), BF16),
                   jax.ShapeDtypeStruct((bsz, GATE_ROWS, s), F32)],
        scratch_shapes=[pltpu.VMEM((tm // nsplit, LANES), F32)],
        compiler_params=_params(("parallel", "parallel")),
        name="mixer_proj",
    )(h, shift, scale, g.reshape(1, d), wn, wt)


def _cmp_kernel(a_ref, pa_ref, pb_ref, w1a_ref, w1b_ref, w2_ref, o_ref, ot_ref):
    nb, ng, n, _ = a_ref.shape
    bias = (jnp.dot(pa_ref[...], w1a_ref[...], preferred_element_type=F32)
            + jnp.dot(pb_ref[...], w1b_ref[...], preferred_element_type=F32))[0:1]
    a = a_ref[...].reshape(nb * ng * n, a_ref.shape[3])
    first = jnp.dot(a, w1a_ref[...], preferred_element_type=F32)
    second = jnp.dot(a, w1b_ref[...], preferred_element_type=F32)
    for i in range(nb * ng):
        rows = slice(i * n, (i + 1) * n)
        hid = first[rows] + pltpu.roll(second[rows], n - 1, axis=0) + bias
        hid = jax.nn.gelu(hid)
        out = jnp.dot(hid.astype(BF16), w2_ref[...], preferred_element_type=F32)
        o_ref[i // ng, i % ng] = out.astype(BF16)
        ot_ref[i // ng, i % ng] = out.T.astype(BF16)


def _cmp_call(kv_chunks, pa, pb, w1a, w1b, w2):
    bsz, _, nchunk, width = kv_chunks.shape
    nb = next(n for n in (4, 2, 1) if bsz % n == 0)
    return pl.pallas_call(
        _cmp_kernel,
        grid=(bsz // nb,),
        in_specs=[pl.BlockSpec((nb, N_KV, nchunk, width), lambda b: (b, 0, 0, 0)),
                  _const_spec(pa.shape), _const_spec(pb.shape),
                  _const_spec(w1a.shape), _const_spec(w1b.shape), _const_spec(w2.shape)],
        out_specs=[pl.BlockSpec((nb, N_KV, nchunk, LANES), lambda b: (b, 0, 0, 0)),
                   pl.BlockSpec((nb, N_KV, LANES, nchunk), lambda b: (b, 0, 0, 0))],
        out_shape=[jax.ShapeDtypeStruct((bsz, N_KV, nchunk, LANES), BF16),
                   jax.ShapeDtypeStruct((bsz, N_KV, LANES, nchunk), BF16)],
        compiler_params=_params(("parallel",)),
        name="nsa_compress",
    )(kv_chunks, pa, pb, w1a, w1b, w2)


POS_HI, POS_LO, POS_ONE = 96, 99, 102
MAX_SEL_BLOCKS = POS_HI


def _position_tables(seq):
    key = np.arange(seq)
    kpos = np.zeros((seq, LANES), np.float32)
    kpos[key, key // SEL_LEN] = 1.0
    kpos[:, POS_HI:POS_HI + 3] = ((key // SEL_LEN) * SEL_LEN)[:, None]
    kpos[:, POS_LO:POS_LO + 3] = (key % SEL_LEN)[:, None]
    kpos[:, POS_ONE:POS_ONE + 3] = 1.0
    ncmp = seq // CMP_STRIDE
    cpos = np.zeros((ncmp, LANES), np.float32)
    cpos[:, POS_HI:POS_HI + 3] = (np.arange(ncmp) * CMP_STRIDE)[:, None]
    cpos[:, POS_LO:POS_LO + 3] = CMP_LEN - 1
    cpos[:, POS_ONE:POS_ONE + 3] = 1.0
    return jnp.asarray(kpos, BF16), jnp.asarray(cpos, BF16)


def _attn_kernel(sinks_ref, qn_ref, qs_ref, gn_ref, kp_ref, vt_ref, kvc_ref, kvct_ref, kpos_ref, cpos_ref,
                 ya_ref, yb_ref, s_scr, m_scr, acc_scr, blocks_ref, *, seq, tq, tqs):
    qi = pl.program_id(1)
    t0 = qi * tq
    cols = N_REP * tq
    nsel = seq // SEL_LEN
    ncmp = kvc_ref.shape[2]
    n_top = min(SEL_TOPN, nsel)

    def stacked_q(q_ref, g, low, width, lane0=0, sel_t=None):
        sub = lax.broadcasted_iota(jnp.int32, (LANES, width), 0)
        zero_q = jnp.zeros((HEAD_DIM, width), BF16)
        blocks = []
        for r in range(N_REP):
            h = g * N_REP + r
            q = q_ref[0, h * HEAD_DIM:(h + 1) * HEAD_DIM, lane0:lane0 + width]
            qpart = jnp.concatenate([q, zero_q] if low else [zero_q, q], axis=0)
            ext = jnp.zeros((LANES, width), F32)
            for i, c in enumerate(SLOPE_TERMS[h]):
                ext = jnp.where((sub == POS_HI + i) | (sub == POS_LO + i), c, ext)
            tpos = (t0 + lane0 + lax.broadcasted_iota(jnp.int32, (1, width), 1)).astype(F32)
            for i, part in enumerate(_split3(-(SLOPES[h] * LOG2E) * tpos)):
                ext = jnp.where(sub == POS_ONE + i, part.astype(F32), ext)
            if sel_t is not None:
                ext = ext + jnp.where(sub < nsel, (sel_t - 1.0) * (-NEG_INF), 0.0)
            blocks.append(jnp.concatenate([qpart, ext.astype(BF16)], axis=0))
        return jnp.concatenate(blocks, axis=1)

    def masked(s, mask, width):
        return jnp.concatenate(
            [jnp.where(mask, s[:, r * width:(r + 1) * width], NEG_INF) for r in range(N_REP)], axis=1)

    def biased(s, bias, width):
        sb = s.astype(BF16)
        return jnp.concatenate([sb[:, r * width:(r + 1) * width] + bias for r in range(N_REP)], axis=1)

    def value_rows(vt):
        return jnp.concatenate([jnp.ones_like(vt), vt], axis=0)

    def load_vt(slot, first_chunk, nchunks):
        return jnp.concatenate([vt_ref[0, slot, first_chunk + j] for j in range(nchunks)], axis=1)

    def band_geometry(window, sub):
        span = tqs + -(-window // KEY_CHUNK) * KEY_CHUNK
        ts = t0 + sub * tqs
        k_start = pl.multiple_of(jnp.maximum(ts + tqs - span, 0), KEY_CHUNK)
        d = ((ts - k_start) + lax.broadcasted_iota(jnp.int32, (span, tqs), 1)
             - lax.broadcasted_iota(jnp.int32, (span, tqs), 0))
        return span, k_start, jnp.where((d >= 0) & (d < window), 0.0, NEG_INF).astype(BF16)

    def band_scores(qt, pair, geometry):
        span, k_start, bias = geometry
        kaug = jnp.concatenate([kp_ref[0, pair, pl.ds(k_start, span), :], kpos_ref[pl.ds(k_start, span), :]], axis=1)
        return biased(jnp.dot(kaug, qt, preferred_element_type=F32), bias, tqs)

    def band_output(s, vslot, geometry, sink_row=None):
        span, k_start, _ = geometry
        m = jnp.max(s, axis=0, keepdims=True)
        if sink_row is not None:
            m = jnp.maximum(m, sink_row)
        p = jnp.exp2(s - m)
        vt = load_vt(vslot, k_start // KEY_CHUNK, span // KEY_CHUNK)
        o = jnp.dot(value_rows(vt), p, preferred_element_type=F32)
        l = o[:HEAD_DIM]
        if sink_row is not None:
            l = l + jnp.exp2(sink_row - m).astype(F32)
        return o[HEAD_DIM:] / l

    tc = t0 + lax.broadcasted_iota(jnp.int32, (ncmp, tq), 1)
    cend = lax.broadcasted_iota(jnp.int32, (ncmp, tq), 0) * CMP_STRIDE + (CMP_LEN - 1)
    cmask = tc >= cend
    row_ok = (t0 + lax.broadcasted_iota(jnp.int32, (1, tq), 1)) >= (CMP_LEN - 1)
    row_ok = jnp.concatenate([row_ok.astype(F32)] * N_REP, axis=1)
    jn = lax.broadcasted_iota(jnp.int32, (nsel, ncmp), 0) * SEL_LEN
    cn = lax.broadcasted_iota(jnp.int32, (nsel, ncmp), 1) * CMP_STRIDE
    ov_t = jnp.where((cn < jn + SEL_LEN) & (cn + CMP_LEN > jn), 1.0, 0.0).astype(BF16)
    jb = lax.broadcasted_iota(jnp.int32, (nsel, tq), 0)
    cur = (t0 + lax.broadcasted_iota(jnp.int32, (nsel, tq), 1)) // SEL_LEN
    forced = (jb == 0) | (jb == cur) | (jb == cur - 1)
    valid = jb <= cur
    cmp_scores = [jnp.dot(jnp.concatenate([kvc_ref[0, g], cpos_ref[...]], axis=1), stacked_q(qn_ref, g, True, tq),
                          preferred_element_type=F32) for g in range(N_KV)]
    o_cmp, scores_sel = [], []
    for g in range(N_KV):
        s = masked(cmp_scores[g], cmask, tq)
        e = jnp.exp2(s - jnp.max(s, axis=0, keepdims=True))
        p = e * (row_ok / jnp.sum(e, axis=0, keepdims=True))
        o_cmp.append(jnp.dot(value_rows(kvct_ref[0, g, HEAD_DIM:, :]), p.astype(BF16),
                             preferred_element_type=F32)[HEAD_DIM:])
        psum = p[:, 0:tq]
        for r in range(1, N_REP):
            psum = psum + p[:, r * tq:(r + 1) * tq]
        imp_t = sum(jnp.dot(ov_t, part, preferred_element_type=F32) for part in _split3(psum))
        scores_sel.append(jnp.where(forced, SEL_BONUS, jnp.where(valid, imp_t, -1.0)))

    items = []
    for sub in range(tq // tqs):
        geo_win = band_geometry(NSA_WINDOW, sub)
        geo_swa = band_geometry(SWA_WINDOW, sub)
        for g in range(N_KV):
            sink_row = jnp.concatenate(
                [jnp.full((1, tqs), sinks_ref[g * N_REP + r] * LOG2E, F32) for r in range(N_REP)],
                axis=1).astype(BF16)
            items.append((qn_ref, g, sub, 1, N_KV + g, geo_win, None))
            items.append((qs_ref, g, sub, 2, 2 * N_KV + g, geo_swa, sink_row))

    def item_scores(item):
        q_ref, g, sub, pair, _, geo, _ = item
        return band_scores(stacked_q(q_ref, g, g == 0, tqs, sub * tqs), pair, geo)

    outs, ahead = [], 2
    pending = [item_scores(items[k]) for k in range(ahead)]
    for k, item in enumerate(items):
        if k + ahead < len(items):
            pending.append(item_scores(items[k + ahead]))
        outs.append(band_output(pending.pop(0), item[4], item[5], item[6]))

    gsig = jax.nn.sigmoid(gn_ref[0])
    partial, yb_cols = [], []
    for sub in range(tq // tqs):
        lane0 = sub * tqs
        gates = gsig[:, lane0:lane0 + tqs]
        rows, yb_rows = [], []
        for g in range(N_KV):
            o_win, o_swa = outs[(sub * N_KV + g) * 2], outs[(sub * N_KV + g) * 2 + 1]
            for r in range(N_REP):
                c = (g * N_REP + r) * 3
                wide = slice(r * tq + lane0, r * tq + lane0 + tqs)
                rows.append(gates[c:c + 1] * o_cmp[g][:, wide] + gates[c + 2:c + 3] * o_win[:, r * tqs:(r + 1) * tqs])
            yb_rows += [o_swa[:, r * tqs:(r + 1) * tqs] for r in range(N_REP)]
        partial.append(rows)
        yb_cols.append(jnp.concatenate(yb_rows, axis=0))
    yb_ref[0] = jnp.concatenate(yb_cols, axis=1).astype(BF16)

    qt_sel, chosen = [], None
    for score in scores_sel:
        rank = [jnp.zeros((8, tq), F32) for _ in range(nsel // 8)]
        for i in range(nsel):
            row = score[i:i + 1, :]
            for k in range(nsel // 8):
                blk = score[8 * k:8 * (k + 1)]
                ge = jnp.where(row >= blk, 1.0, 0.0)
                gt = jnp.where(row > blk, 1.0, 0.0)
                if 8 * k > i:
                    beats = ge
                elif 8 * k + 7 <= i:
                    beats = gt
                else:
                    beats = jnp.where(lax.broadcasted_iota(jnp.int32, (8, tq), 0) > i - 8 * k, ge, gt)
                rank[k] = rank[k] + beats
        sel_t = jnp.where(jnp.concatenate(rank, axis=0) < n_top, 1.0, 0.0)
        sel_t = jnp.concatenate([sel_t, jnp.zeros((LANES - nsel, tq), F32)], axis=0)
        chosen = sel_t if chosen is None else chosen + sel_t
        qt_sel.append(stacked_q(qn_ref, len(qt_sel), len(qt_sel) == 0, tq, 0, sel_t))

    nck = tq // KEY_CHUNK
    per_blk = tq // SEL_LEN
    n_prev = jnp.int32(0)
    for kb in range(seq // tq - 1):
        needed = (jnp.max(chosen[kb * per_blk:(kb + 1) * per_blk, :]) > 0.5) & (kb < qi)
        blocks_ref[n_prev] = kb
        n_prev = n_prev + needed.astype(jnp.int32)
    blocks_ref[n_prev] = qi

    def key_rows(ids):
        starts = [pl.multiple_of(kb * tq, tq) for kb in ids]
        return jnp.concatenate([jnp.concatenate([kp_ref[0, 0, pl.ds(k0, tq), :] for k0 in starts], axis=0),
                                jnp.concatenate([kpos_ref[pl.ds(k0, tq), :] for k0 in starts], axis=0)], axis=1)

    def score_step(ids, slot, bias=None):
        n = len(ids)
        kaug = key_rows(ids)
        raw = [jnp.dot(kaug, qt_sel[g], preferred_element_type=F32) for g in range(N_KV)]
        for g in range(N_KV):
            s = raw[g].astype(BF16) if bias is None else biased(raw[g], bias, tq)
            s_scr[g, pl.ds(slot, n)] = s.reshape(n, tq, cols)
            m_scr[g] = jnp.maximum(m_scr[g], jnp.max(s.reshape(n * tq // 16, 16, cols), axis=0))

    def value_step(ids, slot):
        n = len(ids)
        ps = [jnp.exp2(s_scr[g, pl.ds(slot, n)].reshape(n * tq, cols) - m_scr[g, 0:1, :]) for g in range(N_KV)]
        for g in range(N_KV):
            vt = jnp.concatenate([load_vt(g, kb * nck, nck) for kb in ids], axis=1)
            acc_scr[g] += jnp.dot(value_rows(vt), ps[g], preferred_element_type=F32)

    for g in range(N_KV):
        m_scr[g] = jnp.full((16, cols), NEG_INF, BF16)
        acc_scr[g] = jnp.zeros((LANES, cols), F32)

    def pass1(i, carry):
        score_step((blocks_ref[2 * i], blocks_ref[2 * i + 1]), 2 * i)
        return carry

    lax.fori_loop(0, n_prev // 2, pass1, 0)
    causal = jnp.where(lax.broadcasted_iota(jnp.int32, (tq, tq), 1) >= lax.broadcasted_iota(jnp.int32, (tq, tq), 0),
                       0.0, NEG_INF).astype(BF16)
    odd = (n_prev % 2) == 1

    @pl.when(odd)
    def _():
        score_step((blocks_ref[n_prev - 1], qi), n_prev - 1,
                   jnp.concatenate([jnp.zeros((tq, tq), BF16), causal], axis=0))

    @pl.when(jnp.logical_not(odd))
    def _():
        score_step((qi,), n_prev, causal)

    for g in range(N_KV):
        m_scr[g] = jnp.broadcast_to(jnp.max(m_scr[g], axis=0, keepdims=True), (16, cols))

    def pass2(i, carry):
        value_step((blocks_ref[2 * i], blocks_ref[2 * i + 1]), 2 * i)
        return carry

    lax.fori_loop(0, (n_prev + 1) // 2, pass2, 0)

    @pl.when(jnp.logical_not(odd))
    def _():
        value_step((qi,), n_prev)

    ya_cols = []
    for sub in range(tq // tqs):
        lane0 = sub * tqs
        rows = []
        for g in range(N_KV):
            o_slc = acc_scr[g, HEAD_DIM:, :] / acc_scr[g, :HEAD_DIM, :]
            for r in range(N_REP):
                c = (g * N_REP + r) * 3 + 1
                wide = slice(r * tq + lane0, r * tq + lane0 + tqs)
                rows.append(partial[sub][g * N_REP + r] + gsig[c:c + 1, lane0:lane0 + tqs] * o_slc[:, wide])
        ya_cols.append(jnp.concatenate(rows, axis=0))
    ya_ref[0] = jnp.concatenate(ya_cols, axis=1).astype(BF16)


def _attn_call(sinks, qn_t, qs_t, gn_t, kp, vt, kvc, kvc_t, *, tq=256, tqs=128):
    bsz, _, s = qn_t.shape
    ncmp = kvc.shape[2]
    assert s % tq == 0 and s // SEL_LEN <= MAX_SEL_BLOCKS and s >= tq + NSA_WINDOW
    kpos, cpos = _position_tables(s)
    qspec = pl.BlockSpec((1, Q_COLS, tq), lambda b, i: (b, 0, i))
    yspec = pl.BlockSpec((1, Q_COLS, tq), lambda b, i: (b, 0, i))
    cols = N_REP * tq
    return pl.pallas_call(
        functools.partial(_attn_kernel, seq=s, tq=tq, tqs=tqs),
        grid=(bsz, s // tq),
        in_specs=[pl.BlockSpec(memory_space=pltpu.SMEM),
                  qspec, qspec,
                  pl.BlockSpec((1, GATE_ROWS, tq), lambda b, i: (b, 0, i)),
                  pl.BlockSpec((1, N_KPAIR, s, LANES), lambda b, i: (b, 0, 0, 0)),
                  pl.BlockSpec((1, N_VSLOT, s // KEY_CHUNK, HEAD_DIM, KEY_CHUNK), lambda b, i: (b, 0, 0, 0, 0)),
                  pl.BlockSpec((1, N_KV, ncmp, LANES), lambda b, i: (b, 0, 0, 0)),
                  pl.BlockSpec((1, N_KV, LANES, ncmp), lambda b, i: (b, 0, 0, 0)),
                  _const_spec((s, LANES)), _const_spec((ncmp, LANES))],
        out_specs=[yspec, yspec],
        out_shape=[jax.ShapeDtypeStruct((bsz, Q_COLS, s), BF16),
                   jax.ShapeDtypeStruct((bsz, Q_COLS, s), BF16)],
        scratch_shapes=[pltpu.VMEM((N_KV, s // tq, tq, cols), BF16),
                        pltpu.VMEM((N_KV, 16, cols), BF16),
                        pltpu.VMEM((N_KV, LANES, cols), F32),
                        pltpu.SMEM((s // tq + 1,), jnp.int32)],
        compiler_params=_params(("parallel", "arbitrary")),
        name="hybrid_attention",
    )(sinks, qn_t, qs_t, gn_t, kp, vt, kvc, kvc_t, kpos, cpos)


def _merge_kernel(x_ref, sh_ref, sc_ref, gt_ref, g_ref, ya_ref, yb_ref, wgm_ref, wa_ref, wb_ref, wo_ref, o_ref, *,
                  nsplit):
    d = x_ref.shape[-1]
    rows = x_ref.shape[1] // nsplit

    def products(h):
        seg = slice(h * rows, (h + 1) * rows)
        u = _modulated_norm(x_ref[0, seg], g_ref[...], sh_ref[0], sc_ref[0]).astype(BF16)
        tn = (((0,), (0,)), ((), ()))
        return (lax.dot_general(ya_ref[0, :, seg], wa_ref[...].astype(BF16), tn, preferred_element_type=F32),
                jnp.dot(u, wgm_ref[:, :d], preferred_element_type=F32),
                lax.dot_general(yb_ref[0, :, seg], wb_ref[...].astype(BF16), tn, preferred_element_type=F32),
                jnp.dot(u, wgm_ref[:, d:], preferred_element_type=F32))

    nxt = products(0)
    for h in range(nsplit):
        up_a, gate_a, up_b, gate_b = nxt
        if h + 1 < nsplit:
            nxt = products(h + 1)
        merged = jax.nn.sigmoid(gate_a) * up_a + jax.nn.sigmoid(gate_b) * up_b
        y = jnp.dot(merged.astype(BF16), wo_ref[...].astype(BF16), preferred_element_type=F32)
        seg = slice(h * rows, (h + 1) * rows)
        o_ref[0, seg] = x_ref[0, seg] + gt_ref[0] * y


def _merge_call(h, shift, scale, gate, g, ya, yb, wgm, wa, wb, wo, *, tm=1024, nsplit=4):
    bsz, s, d = h.shape
    vec = pl.BlockSpec((1, 1, d), lambda b, i: (b, 0, 0))
    tok = pl.BlockSpec((1, tm, d), lambda b, i: (b, i, 0))
    ysp = pl.BlockSpec((1, Q_COLS, tm), lambda b, i: (b, 0, i))
    return pl.pallas_call(
        functools.partial(_merge_kernel, nsplit=nsplit),
        grid=(bsz, s // tm),
        in_specs=[tok, vec, vec, vec, _const_spec((1, d)), ysp, ysp,
                  _const_spec(wgm.shape), _const_spec(wa.shape), _const_spec(wb.shape), _const_spec(wo.shape)],
        out_specs=tok,
        out_shape=jax.ShapeDtypeStruct((bsz, s, d), F32),
        compiler_params=_params(("parallel", "parallel")),
        name="mixer_merge",
    )(h, shift, scale, gate, g.reshape(1, d), ya, yb, wgm, wa, wb, wo)


def _proj_column_ranges():
    kvw = 2 * N_KV * HEAD_DIM
    off_qn = 0
    off_c = off_qn + Q_COLS
    off_s = off_c + kvw
    off_w = off_s + kvw
    off_gn = off_w + kvw
    off_qs = off_gn + 3 * N_HEADS
    off_b = off_qs + Q_COLS
    off_gm = off_b + kvw
    half = N_KV * HEAD_DIM
    nn = [(off, off + half) for off in (off_s, off_w, off_b)]
    for g in range(N_KV):
        nn += [(off_c + g * HEAD_DIM, off_c + (g + 1) * HEAD_DIM),
               (off_c + half + g * HEAD_DIM, off_c + half + (g + 1) * HEAD_DIM)]
    tt = [(off_qn, off_qn + Q_COLS), (off_qs, off_qs + Q_COLS)]
    tt += [(off + half, off + 2 * half) for off in (off_s, off_w, off_b)]
    tt += [(off_gn, off_gn + 3 * N_HEADS)]
    return nn, tt, off_gm


def _take_columns(w, ranges, width):
    parts = [w[:, a:b] for a, b in ranges]
    have = sum(b - a for a, b in ranges)
    if width > have:
        parts.append(jnp.zeros((w.shape[0], width - have), w.dtype))
    return jnp.concatenate(parts, axis=1)


def _compress_weights(pos_k, w1_k, w2_k, pos_v, w1_v, w2_v):
    half = CMP_LEN // 2
    zk = jnp.zeros((half, HEAD_DIM, CMP_HIDDEN), F32)

    def w1_half(sl):
        wk = jnp.concatenate([w1_k[sl], zk], axis=-1)
        wv = jnp.concatenate([zk, w1_v[sl]], axis=-1)
        return jnp.concatenate([wk, wv], axis=1).reshape(half * 2 * HEAD_DIM, 2 * CMP_HIDDEN).astype(BF16)

    def pos_half(sl):
        p = jnp.concatenate([pos_k[sl], pos_v[sl]], axis=1).reshape(1, half * 2 * HEAD_DIM)
        return jnp.broadcast_to(p, (8, p.shape[1])).astype(BF16)

    z2 = jnp.zeros((CMP_HIDDEN, HEAD_DIM), F32)
    w2 = jnp.concatenate([jnp.concatenate([w2_k, z2], axis=1),
                          jnp.concatenate([z2, w2_v], axis=1)], axis=0).astype(BF16)
    lo, hi = slice(0, half), slice(half, CMP_LEN)
    return pos_half(lo), pos_half(hi), w1_half(lo), w1_half(hi), w2


def kernel(x, c, w_ada, b_ada, g_ffn1, w1_gate, w1_up, w1_down, g_mix, w_in, cmp_pos_k, cmp_w1_k, cmp_w2_k,
           cmp_pos_v, cmp_w1_v, cmp_w2_v, sinks, w_up_a, w_up_b, w_out, g_ffn2, w2_gate, w2_up, w2_down, g_final):
    bsz, seq, d = x.shape
    depth = w_ada.shape[0]
    nn_ranges, t_ranges, off_gm = _proj_column_ranges()
    h = x
    for l in range(depth):
        mod = _ada_call(c, w_ada[l], b_ada[l])
        sh1, sc1, gt1, sh2, sc2, gt2, sh3, sc3, gt3 = [m.reshape(bsz, 1, d) for m in jnp.split(mod, 9, axis=-1)]
        last = l == depth - 1

        h = _ffn_call(h, sh1, sc1, gt1, g_ffn1[l], w1_gate[l], w1_up[l], w1_down[l])

        w_nn = _take_columns(w_in[l], nn_ranges, NN_COLS).astype(BF16)
        w_t = _take_columns(w_in[l], t_ranges, T_ROWS).T.astype(BF16)
        kp, kvc_in, qn_t, qs_t, vt, gn_t = _proj_call(h, sh2, sc2, g_mix[l], w_nn, w_t)

        pa, pb, w1a, w1b, w2c = _compress_weights(cmp_pos_k[l], cmp_w1_k[l], cmp_w2_k[l],
                                                  cmp_pos_v[l], cmp_w1_v[l], cmp_w2_v[l])
        kvc, kvc_t = _cmp_call(kvc_in, pa, pb, w1a, w1b, w2c)

        ya, yb = _attn_call(sinks[l].reshape(-1), qn_t, qs_t, gn_t, kp, vt, kvc, kvc_t)

        h = _merge_call(h, sh2, sc2, gt2, g_mix[l], ya, yb,
                        w_in[l][:, off_gm:].astype(BF16), w_up_a[l], w_up_b[l], w_out[l])

        h = _ffn_call(h, sh3, sc3, gt3, g_ffn2[l], w2_gate[l], w2_up[l], w2_down[l],
                      g_final if last else None)
    return h
```
